```python
import jax, jax.numpy as jnp
from jax import lax
import numpy as np

D_MODEL = 2048
BATCH = 1
SEQ = 8192
DEPTH = 4

N_MIXERS = 4
N_HEADS = 16
HEAD_DIM = 128
MIX_WIDTH = N_HEADS * HEAD_DIM
Q_BLOCK = 128
EPS = 1e-6
NEG_BIG = -1e30

MLA_Q_RANK = 512
MLA_KV_RANK = 256
MLA_NOPE = 128
MLA_ROPE = 64
MLA_V = 128
MLA_QK_DIM = MLA_NOPE + MLA_ROPE
ROPE_THETA = 10000.0
MLA_IN = MLA_Q_RANK + MLA_KV_RANK + MLA_ROPE + MIX_WIDTH

MOBA_BLOCK = 256
MOBA_TOPK = 3
MOBA_Q_CHUNK = 16

QKV_GATE_IN = 4 * MIX_WIDTH

NSA_GROUPS = 4
NSA_HPG = N_HEADS // NSA_GROUPS
NSA_CMP_BLOCK = 32
NSA_CMP_STRIDE = 16
NSA_SEL_BLOCK = 64
NSA_SEL_N = 16
NSA_WINDOW = 512
NSA_Q_CHUNK = 32
NSA_KV_WIDTH = 2 * NSA_GROUPS * HEAD_DIM
NSA_IN = MIX_WIDTH + 3 * NSA_KV_WIDTH + 3 * N_HEADS + MIX_WIDTH

kernel_name = 'hybrid_mla_moba_stickbreak_nsa'


def layers_of(m):
    return len(range(m, DEPTH, N_MIXERS))


def rms_norm(x, g):
    xf = x.astype(jnp.float32)
    y = xf * lax.rsqrt(jnp.mean(xf * xf, axis=-1, keepdims=True) + EPS)
    return (y * g.astype(jnp.float32)).astype(x.dtype)


def alibi_slopes():
    return jnp.asarray(2.0 ** (-8.0 * np.arange(1, N_HEADS + 1) / N_HEADS), dtype=jnp.float32)


def to_heads(t, n, d):
    B, S, _ = t.shape
    return t.reshape(B, S, n, d).transpose(0, 2, 1, 3)


def from_heads(t):
    B, n, S, d = t.shape
    return t.transpose(0, 2, 1, 3).reshape(B, S, n * d)


def rope(x):
    S, R = x.shape[1], x.shape[-1]
    inv_freq = ROPE_THETA ** (-jnp.arange(0, R, 2, dtype=jnp.float32) / R)
    ang = jnp.arange(S, dtype=jnp.float32)[:, None] * inv_freq[None, :]
    cos = jnp.cos(ang)[None, :, None, :]
    sin = jnp.sin(ang)[None, :, None, :]
    x1, x2 = jnp.split(x.astype(jnp.float32), 2, axis=-1)
    return jnp.concatenate([x1 * cos - x2 * sin, x1 * sin + x2 * cos], axis=-1).astype(x.dtype)


def causal_block_attention(q, k, v, scale):
    B, H, S, dk = q.shape
    nq = S // Q_BLOCK
    qb = q.reshape(B, H, nq, Q_BLOCK, dk).transpose(2, 0, 1, 3, 4)
    k_pos = jnp.arange(S)

    def one(args):
        qi, i = args
        q_pos = i * Q_BLOCK + jnp.arange(Q_BLOCK)
        s = jnp.einsum('bhqd,bhkd->bhqk', qi, k).astype(jnp.float32) * scale
        s = jnp.where(k_pos[None, :] <= q_pos[:, None], s, -jnp.inf)
        p = jax.nn.softmax(s, axis=-1)
        return jnp.einsum('bhqk,bhkd->bhqd', p.astype(v.dtype), v)

    o = lax.map(one, (qb, jnp.arange(nq)))
    return o.transpose(1, 2, 0, 3, 4).reshape(B, H, S, v.shape[-1])


def mla_mixer(h, w_in, q_lat_norm, w_q_up, kv_lat_norm, w_kv_up, q_norm, k_norm):
    B, S, _ = h.shape
    proj = h @ w_in
    c1 = MLA_Q_RANK
    c2 = c1 + MLA_KV_RANK
    c3 = c2 + MLA_ROPE
    q_lat, kv_lat, k_rope, gate = proj[..., :c1], proj[..., c1:c2], proj[..., c2:c3], proj[..., c3:]
    q = (rms_norm(q_lat, q_lat_norm) @ w_q_up).reshape(B, S, N_HEADS, MLA_QK_DIM)
    kv = (rms_norm(kv_lat, kv_lat_norm) @ w_kv_up).reshape(B, S, N_HEADS, MLA_NOPE + MLA_V)
    k_nope, v = kv[..., :MLA_NOPE], kv[..., MLA_NOPE:]
    q = jnp.concatenate([rms_norm(q[..., :MLA_NOPE], q_norm[:MLA_NOPE]),
                         rope(rms_norm(q[..., MLA_NOPE:], q_norm[MLA_NOPE:]))], axis=-1)
    k_nope = rms_norm(k_nope, k_norm[:MLA_NOPE])
    k_rope = rope(rms_norm(k_rope, k_norm[MLA_NOPE:])[:, :, None, :])
    k = jnp.concatenate([k_nope, jnp.broadcast_to(k_rope, (B, S, N_HEADS, MLA_ROPE))], axis=-1)
    o = causal_block_attention(q.transpose(0, 2, 1, 3), k.transpose(0, 2, 1, 3),
                               v.transpose(0, 2, 1, 3), MLA_QK_DIM ** -0.5)
    return from_heads(o) * jax.nn.silu(gate)


def moba_attention(q, k, v, slopes):
    B, H, S, D = q.shape
    Sp = -(-S // MOBA_BLOCK) * MOBA_BLOCK
    pad = ((0, 0), (0, 0), (0, Sp - S), (0, 0))
    q, k, v = jnp.pad(q, pad), jnp.pad(k, pad), jnp.pad(v, pad)
    NB = Sp // MOBA_BLOCK
    kb = k.reshape(B, H, NB, MOBA_BLOCK, D)
    vb = v.reshape(B, H, NB, MOBA_BLOCK, D)
    k_mean = jnp.mean(kb.astype(jnp.float32), axis=3)
    score = jnp.einsum('bhsd,bhnd->bhsn', q.astype(jnp.float32), k_mean)
    q_blk = jnp.arange(Sp) // MOBA_BLOCK
    past = jnp.arange(NB)[None, :] < q_blk[:, None]
    score = jnp.where(past, score, -jnp.inf)
    kk = min(MOBA_TOPK, NB)
    top_val, top_idx = lax.top_k(score, kk)
    top_ok = top_val > -jnp.inf

    Cq = MOBA_Q_CHUNK
    nc = Sp // Cq
    qc = q.reshape(B, H, nc, Cq, D).transpose(2, 0, 1, 3, 4)
    ic = top_idx.reshape(B, H, nc, Cq, kk).transpose(2, 0, 1, 3, 4)
    okc = top_ok.reshape(B, H, nc, Cq, kk).transpose(2, 0, 1, 3, 4)
    bi = jnp.arange(B)[:, None, None, None]
    hi = jnp.arange(H)[None, :, None, None]
    sl = slopes[None, :, None, None]
    scale = D ** -0.5
    offs = jnp.arange(MOBA_BLOCK)

    def one(args):
        qi, idx, ok, c = args
        q_pos = c * Cq + jnp.arange(Cq)
        own = (c * Cq) // MOBA_BLOCK
        k_sel = kb[bi, hi, idx].reshape(B, H, Cq, kk * MOBA_BLOCK, D)
        v_sel = vb[bi, hi, idx].reshape(B, H, Cq, kk * MOBA_BLOCK, D)
        sel_pos = (idx[..., None] * MOBA_BLOCK + offs).reshape(B, H, Cq, kk * MOBA_BLOCK)
        sel_ok = jnp.repeat(ok, MOBA_BLOCK, axis=-1)
        d_sel = (q_pos[:, None] - sel_pos).astype(jnp.float32)
        s_sel = jnp.einsum('bhqd,bhqkd->bhqk', qi, k_sel).astype(jnp.float32) * scale - sl * d_sel
        s_sel = jnp.where(sel_ok, s_sel, -jnp.inf)
        k_own = lax.dynamic_index_in_dim(kb, own, axis=2, keepdims=False)
        v_own = lax.dynamic_index_in_dim(vb, own, axis=2, keepdims=False)
        d_own = q_pos[:, None] - (own * MOBA_BLOCK + offs)[None, :]
        s_own = jnp.einsum('bhqd,bhkd->bhqk', qi, k_own).astype(jnp.float32) * scale - sl * d_own.astype(jnp.float32)
        s_own = jnp.where(d_own >= 0, s_own, -jnp.inf)
        p = jax.nn.softmax(jnp.concatenate([s_sel, s_own], axis=-1), axis=-1).astype(v.dtype)
        n_sel = kk * MOBA_BLOCK
        return (jnp.einsum('bhqk,bhqkd->bhqd', p[..., :n_sel], v_sel)
                + jnp.einsum('bhqk,bhkd->bhqd', p[..., n_sel:], v_own))

    o = lax.map(one, (qc, ic, okc, jnp.arange(nc)))
    return o.transpose(1, 2, 0, 3, 4).reshape(B, H, Sp, D)[:, :, :S]


def moba_mixer(h, w_in, q_norm, k_norm):
    q, k, v, gate = jnp.split(h @ w_in, 4, axis=-1)
    q = rms_norm(to_heads(q, N_HEADS, HEAD_DIM), q_norm)
    k = rms_norm(to_heads(k, N_HEADS, HEAD_DIM), k_norm)
    v = to_heads(v, N_HEADS, HEAD_DIM)
    o = moba_attention(q, k, v, alibi_slopes())
    return from_heads(o) * jax.nn.silu(gate)


def stick_breaking_attention(q, k, v):
    B, H, S, D = q.shape
    nq = S // Q_BLOCK
    qb = q.reshape(B, H, nq, Q_BLOCK, D).transpose(2, 0, 1, 3, 4)
    k_pos = jnp.arange(S)
    scale = D ** -0.5

    def one(args):
        qi, i = args
        q_pos = i * Q_BLOCK + jnp.arange(Q_BLOCK)
        before = k_pos[None, :] < q_pos[:, None]
        z = jnp.einsum('bhqd,bhkd->bhqk', qi, k).astype(jnp.float32) * scale
        log_keep = jnp.where(before, jax.nn.log_sigmoid(-z), 0.0)
        log_between = lax.cumsum(log_keep, axis=3, reverse=True) - log_keep
        a = jnp.where(before, jnp.exp(jax.nn.log_sigmoid(z) + log_between), 0.0)
        return jnp.einsum('bhqk,bhkd->bhqd', a.astype(v.dtype), v)

    o = lax.map(one, (qb, jnp.arange(nq)))
    return o.transpose(1, 2, 0, 3, 4).reshape(B, H, S, D)


def stick_breaking_mixer(h, w_in):
    q, k, v, gate = jnp.split(h @ w_in, 4, axis=-1)
    o = stick_breaking_attention(to_heads(q, N_HEADS, HEAD_DIM), to_heads(k, N_HEADS, HEAD_DIM),
                                 to_heads(v, N_HEADS, HEAD_DIM))
    return from_heads(o) * jax.nn.silu(gate)


def nsa_compress(t, w, pos):
    S = t.shape[2]
    n_cmp = (S - NSA_CMP_BLOCK) // NSA_CMP_STRIDE + 1
    idx = np.arange(n_cmp)[:, None] * NSA_CMP_STRIDE + np.arange(NSA_CMP_BLOCK)[None, :]
    blocks = t[:, :, idx] + pos
    return blocks.reshape(blocks.shape[0], blocks.shape[1], n_cmp, -1) @ w


def nsa_compressed_branch(q, k_cmp, v_cmp, slopes):
    B, G, P, S, D = q.shape
    n_cmp = k_cmp.shape[2]
    n_sel = S // NSA_SEL_BLOCK
    kn = min(NSA_SEL_N, n_sel)
    cmp_end = jnp.arange(n_cmp) * NSA_CMP_STRIDE + NSA_CMP_BLOCK - 1
    c_start = np.arange(n_cmp)[:, None] * NSA_CMP_STRIDE
    s_start = np.arange(n_sel)[None, :] * NSA_SEL_BLOCK
    cmp_to_sel = jnp.asarray((c_start < s_start + NSA_SEL_BLOCK) & (c_start + NSA_CMP_BLOCK > s_start),
                             dtype=jnp.float32)
    sl = slopes[None, :, :, None, None]
    scale = D ** -0.5
    nq = S // Q_BLOCK
    qb = q.reshape(B, G, P, nq, Q_BLOCK, D).transpose(3, 0, 1, 2, 4, 5)
    blk_ids = jnp.arange(n_sel)

    def one(args):
        qi, i = args
        q_pos = i * Q_BLOCK + jnp.arange(Q_BLOCK)
        dist = q_pos[:, None] - cmp_end[None, :]
        mask = dist >= 0
        s = jnp.einsum('bgpqd,bgnd->bgpqn', qi, k_cmp).astype(jnp.float32) * scale - sl * dist.astype(jnp.float32)
        p = jax.nn.softmax(jnp.where(mask, s, NEG_BIG), axis=-1) * mask
        o = jnp.einsum('bgpqn,bgnd->bgpqd', p.astype(v_cmp.dtype), v_cmp)
        imp = jnp.einsum('bgpqn,nj->bgqj', p, cmp_to_sel)
        cur = (q_pos // NSA_SEL_BLOCK)[:, None]
        forced = (blk_ids == 0) | (blk_ids == cur) | (blk_ids == cur - 1)
        imp = jnp.where(blk_ids > cur, -jnp.inf, jnp.where(forced, jnp.inf, imp))
        val, idx = lax.top_k(imp, kn)
        return o, idx, val > -jnp.inf

    o, idx, ok = lax.map(one, (qb, jnp.arange(nq)))
    o = o.transpose(1, 2, 3, 0, 4, 5).reshape(B, G, P, S, D)
    idx = idx.transpose(1, 2, 0, 3, 4).reshape(B, G, S, kn)
    ok = ok.transpose(1, 2, 0, 3, 4).reshape(B, G, S, kn)
    return o, idx, ok


def nsa_selected_branch(q, k, v, sel_idx, sel_ok, slopes):
    B, G, P, S, D = q.shape
    n_sel = S // NSA_SEL_BLOCK
    kn = sel_idx.shape[-1]
    kb = k.reshape(B, G, n_sel, NSA_SEL_BLOCK, D)
    vb = v.reshape(B, G, n_sel, NSA_SEL_BLOCK, D)
    Cq = NSA_Q_CHUNK
    nc = S // Cq
    qc = q.reshape(B, G, P, nc, Cq, D).transpose(3, 0, 1, 2, 4, 5)
    ic = sel_idx.reshape(B, G, nc, Cq, kn).transpose(2, 0, 1, 3, 4)
    okc = sel_ok.reshape(B, G, nc, Cq, kn).transpose(2, 0, 1, 3, 4)
    bi = jnp.arange(B)[:, None, None, None]
    gi = jnp.arange(G)[None, :, None, None]
    sl = slopes[None, :, :, None, None]
    scale = D ** -0.5
    offs = jnp.arange(NSA_SEL_BLOCK)
    n_keys = kn * NSA_SEL_BLOCK

    def one(args):
        qi, idx, ok, c = args
        q_pos = c * Cq + jnp.arange(Cq)
        k_sel = kb[bi, gi, idx].reshape(B, G, Cq, n_keys, D)
        v_sel = vb[bi, gi, idx].reshape(B, G, Cq, n_keys, D)
        k_pos = (idx[..., None] * NSA_SEL_BLOCK + offs).reshape(B, G, Cq, n_keys)
        dist = q_pos[:, None] - k_pos
        mask = (dist >= 0) & jnp.repeat(ok, NSA_SEL_BLOCK, axis=-1)
        s = jnp.einsum('bgpqd,bgqkd->bgpqk', qi, k_sel).astype(jnp.float32) * scale - sl * dist[:, :, None].astype(jnp.float32)
        p = jax.nn.softmax(jnp.where(mask[:, :, None], s, -jnp.inf), axis=-1)
        return jnp.einsum('bgpqk,bgqkd->bgpqd', p.astype(v.dtype), v_sel)

    o = lax.map(one, (qc, ic, okc, jnp.arange(nc)))
    return o.transpose(1, 2, 3, 0, 4, 5).reshape(B, G, P, S, D)


def nsa_window_branch(q, k, v, slopes):
    B, G, P, S, D = q.shape
    W = NSA_WINDOW
    L = W + Q_BLOCK
    pad = ((0, 0), (0, 0), (W, 0), (0, 0))
    kp, vp = jnp.pad(k, pad), jnp.pad(v, pad)
    sl = slopes[None, :, :, None, None]
    scale = D ** -0.5
    nq = S // Q_BLOCK
    qb = q.reshape(B, G, P, nq, Q_BLOCK, D).transpose(3, 0, 1, 2, 4, 5)

    def one(args):
        qi, i = args
        qs = i * Q_BLOCK
        kw = lax.dynamic_slice_in_dim(kp, qs, L, axis=2)
        vw = lax.dynamic_slice_in_dim(vp, qs, L, axis=2)
        k_pos = qs - W + jnp.arange(L)
        q_pos = qs + jnp.arange(Q_BLOCK)
        dist = q_pos[:, None] - k_pos[None, :]
        mask = (dist >= 0) & (dist < W) & (k_pos[None, :] >= 0)
        s = jnp.einsum('bgpqd,bgkd->bgpqk', qi, kw).astype(jnp.float32) * scale - sl * dist.astype(jnp.float32)
        p = jax.nn.softmax(jnp.where(mask, s, -jnp.inf), axis=-1)
        return jnp.einsum('bgpqk,bgkd->bgpqd', p.astype(v.dtype), vw)

    o = lax.map(one, (qb, jnp.arange(nq)))
    return o.transpose(1, 2, 3, 0, 4, 5).reshape(B, G, P, S, D)


def nsa_mixer(h, w_in, q_norm, k_norm, w_cmp_k, w_cmp_v, cmp_pos):
    B, S, _ = h.shape
    G, P, D = NSA_GROUPS, NSA_HPG, HEAD_DIM
    cuts = np.cumsum([MIX_WIDTH, NSA_KV_WIDTH, NSA_KV_WIDTH, NSA_KV_WIDTH, 3 * N_HEADS]).tolist()
    q, kv_c, kv_s, kv_w, g_br, gate = jnp.split(h @ w_in, cuts, axis=-1)
    q = rms_norm(to_heads(q, N_HEADS, D), q_norm).reshape(B, G, P, S, D)

    def kv_split(t):
        kk, vv = jnp.split(t, 2, axis=-1)
        return to_heads(kk, G, D), to_heads(vv, G, D)

    kc, vc = kv_split(kv_c)
    k_cmp = rms_norm(nsa_compress(kc, w_cmp_k, cmp_pos), k_norm[0])
    v_cmp = nsa_compress(vc, w_cmp_v, cmp_pos)
    ks, vs = kv_split(kv_s)
    ks = rms_norm(ks, k_norm[1])
    kw, vw = kv_split(kv_w)
    kw = rms_norm(kw, k_norm[2])
    slopes = alibi_slopes().reshape(G, P)
    o_cmp, sel_idx, sel_ok = nsa_compressed_branch(q, k_cmp, v_cmp, slopes)
    o_slc = nsa_selected_branch(q, ks, vs, sel_idx, sel_ok, slopes)
    o_win = nsa_window_branch(q, kw, vw, slopes)
    g = jax.nn.sigmoid(g_br.astype(jnp.float32)).reshape(B, S, G, P, 3).transpose(4, 0, 2, 3, 1)[..., None]
    o = (g[0] * o_cmp + g[1] * o_slc + g[2] * o_win).astype(h.dtype)
    return from_heads(o.reshape(B, N_HEADS, S, D)) * jax.nn.silu(gate)


def setup_inputs(seed: int = 0) -> dict:
    key = jax.random.key(seed)
    keys = iter(jax.random.split(key, 32))

    def dense(shape, fan_in):
        return jax.random.normal(next(keys), shape, jnp.float32) * (fan_in ** -0.5)

    def gain(shape):
        return 1.0 + 0.02 * jax.random.normal(next(keys), shape, jnp.float32)

    na, nb, nc, nd = (layers_of(m) for m in range(N_MIXERS))
    D = D_MODEL
    cmp_in = NSA_CMP_BLOCK * HEAD_DIM
    return {
        'x': jax.random.normal(next(keys), (BATCH, SEQ, D), jnp.float32),
        'norm_a': gain((na, D)),
        'w_in_a': dense((na, D, MLA_IN), D),
        'q_lat_norm_a': gain((na, MLA_Q_RANK)),
        'w_q_up_a': dense((na, MLA_Q_RANK, N_HEADS * MLA_QK_DIM), MLA_Q_RANK),
        'kv_lat_norm_a': gain((na, MLA_KV_RANK)),
        'w_kv_up_a': dense((na, MLA_KV_RANK, N_HEADS * (MLA_NOPE + MLA_V)), MLA_KV_RANK),
        'q_norm_a': gain((na, MLA_QK_DIM)),
        'k_norm_a': gain((na, MLA_QK_DIM)),
        'w_out_a': dense((na, MIX_WIDTH, D), MIX_WIDTH),
        'norm_b': gain((nb, D)),
        'w_in_b': dense((nb, D, QKV_GATE_IN), D),
        'q_norm_b': gain((nb, HEAD_DIM)),
        'k_norm_b': gain((nb, HEAD_DIM)),
        'w_out_b': dense((nb, MIX_WIDTH, D), MIX_WIDTH),
        'norm_c': gain((nc, D)),
        'w_in_c': dense((nc, D, QKV_GATE_IN), D),
        'w_out_c': dense((nc, MIX_WIDTH, D), MIX_WIDTH),
        'norm_d': gain((nd, D)),
        'w_in_d': dense((nd, D, NSA_IN), D),
        'q_norm_d': gain((nd, HEAD_DIM)),
        'k_norm_d': gain((nd, 3, HEAD_DIM)),
        'w_cmp_k_d': dense((nd, cmp_in, HEAD_DIM), cmp_in),
        'w_cmp_v_d': dense((nd, cmp_in, HEAD_DIM), cmp_in),
        'cmp_pos_d': 0.1 * jax.random.normal(next(keys), (nd, NSA_CMP_BLOCK, HEAD_DIM), jnp.float32),
        'w_out_d': dense((nd, MIX_WIDTH, D), MIX_WIDTH),
    }


def reference(x, norm_a, w_in_a, q_lat_norm_a, w_q_up_a, kv_lat_norm_a, w_kv_up_a, q_norm_a, k_norm_a, w_out_a,
              norm_b, w_in_b, q_norm_b, k_norm_b, w_out_b,
              norm_c, w_in_c, w_out_c,
              norm_d, w_in_d, q_norm_d, k_norm_d, w_cmp_k_d, w_cmp_v_d, cmp_pos_d, w_out_d):
    for i in range(DEPTH):
        m, j = i % N_MIXERS, i // N_MIXERS
        if m == 0:
            y = mla_mixer(rms_norm(x, norm_a[j]), w_in_a[j], q_lat_norm_a[j], w_q_up_a[j],
                          kv_lat_norm_a[j], w_kv_up_a[j], q_norm_a[j], k_norm_a[j]) @ w_out_a[j]
        elif m == 1:
            y = moba_mixer(rms_norm(x, norm_b[j]), w_in_b[j], q_norm_b[j], k_norm_b[j]) @ w_out_b[j]
        elif m == 2:
            y = stick_breaking_mixer(rms_norm(x, norm_c[j]), w_in_c[j]) @ w_out_c[j]
        else:
            y = nsa_mixer(rms_norm(x, norm_d[j]), w_in_d[j], q_norm_d[j], k_norm_d[j],
                          w_cmp_k_d[j], w_cmp_v_d[j], cmp_pos_d[j]) @ w_out_d[j]
        x = x + y
    return x
```

```python
import functools

import numpy as np
import jax
import jax.numpy as jnp
from jax import lax
from jax.experimental import pallas as pl
from jax.experimental.pallas import tpu as pltpu

N_HEADS = 16
HEAD_DIM = 128
MIX_WIDTH = N_HEADS * HEAD_DIM
EPS = 1e-6
NEG_BIG = -1e30
LANES = 128

MLA_Q_RANK = 512
MLA_KV_RANK = 256
MLA_NOPE = 128
MLA_ROPE = 64
MLA_QK_DIM = MLA_NOPE + MLA_ROPE
ROPE_THETA = 10000.0

MOBA_BLOCK = 256
MOBA_TOPK = 3

NSA_GROUPS = 4
NSA_HPG = N_HEADS // NSA_GROUPS
NSA_CMP_BLOCK = 32
NSA_CMP_STRIDE = 16
NSA_SEL_BLOCK = 64
NSA_SEL_N = 16
NSA_WINDOW = 512
NSA_KV_WIDTH = 2 * NSA_GROUPS * HEAD_DIM

SB_LOG_CUTOFF = -110.0

VMEM_LIMIT = 56 * 1024 * 1024

BF16 = jnp.bfloat16
F32 = jnp.float32


def _params(*sem):
    return pltpu.CompilerParams(dimension_semantics=sem, vmem_limit_bytes=VMEM_LIMIT)


def _alibi_slopes():
    return 2.0 ** (-8.0 * np.arange(1, N_HEADS + 1) / N_HEADS)


def _rmsnorm_body(x_ref, g_ref, o_ref):
    x = x_ref[...]
    ms = jnp.mean(x * x, axis=-1, keepdims=True)
    o_ref[...] = (x * lax.rsqrt(ms + EPS) * g_ref[...]).astype(o_ref.dtype)


def rmsnorm_rows(x, g, tm=512):
    S, D = x.shape
    return pl.pallas_call(
        _rmsnorm_body,
        out_shape=jax.ShapeDtypeStruct((S, D), BF16),
        grid=(S // tm,),
        in_specs=[pl.BlockSpec((tm, D), lambda i: (i, 0)),
                  pl.BlockSpec((1, D), lambda i: (0, 0))],
        out_specs=pl.BlockSpec((tm, D), lambda i: (i, 0)),
        compiler_params=_params("parallel"),
        name="rmsnorm_rows",
    )(x, g.reshape(1, D))


def _head_rms(y, gain, n_real):
    ms = jnp.sum(y * y, axis=-1, keepdims=True) * (1.0 / n_real)
    return y * lax.rsqrt(ms + EPS) * gain


def _epi_none(acc, scale):
    return acc * scale if scale != 1.0 else acc


def _epi_silu(acc):
    return acc / (1.0 + jnp.exp(-acc))


def _epi_sigmoid(acc):
    return 1.0 / (1.0 + jnp.exp(-acc))


def _epi_rownorm(acc, g_ref):
    return _head_rms(acc, g_ref[...], acc.shape[-1])


def _epi_headnorm(acc, g_ref, scale):
    g = g_ref[...] * scale
    parts = [_head_rms(acc[:, c:c + LANES], g, LANES) for c in range(0, acc.shape[-1], LANES)]
    return parts[0] if len(parts) == 1 else jnp.concatenate(parts, axis=-1)


def _rope_slab(y, g_ref, cos_ref, sin_ref, scale):
    y = _head_rms(y, g_ref[...], MLA_ROPE)
    return (y * cos_ref[...] + pltpu.roll(y, 64, axis=1) * sin_ref[...]) * scale


def _epi_mla_q(acc, gn_ref, gr_ref, cos_ref, sin_ref, scale):
    parts = []
    gn = gn_ref[...] * scale
    for c in range(0, acc.shape[-1], 2 * LANES):
        parts.append(_head_rms(acc[:, c:c + LANES], gn, MLA_NOPE))
        parts.append(_rope_slab(acc[:, c + LANES:c + 2 * LANES], gr_ref, cos_ref, sin_ref, scale))
    return jnp.concatenate(parts, axis=-1)


def _epi_mla_krope(acc, gr_ref, cos_ref, sin_ref):
    return _rope_slab(acc, gr_ref, cos_ref, sin_ref, 1.0)


def _proj_body(epi, n_extra, h_ref, w_ref, *refs):
    extra, o_ref = refs[:n_extra], refs[n_extra]
    acc = jnp.dot(h_ref[...], w_ref[...], preferred_element_type=F32)
    o_ref[...] = epi(acc, *extra).astype(o_ref.dtype)


def proj(h, w, epi, extra=(), *, tm=512, tn=None, out_dtype=BF16, name="proj"):
    S, K = h.shape
    N = w.shape[1]
    tn = N if tn is None else tn
    in_specs = [pl.BlockSpec((tm, K), lambda i, j: (i, 0)),
                pl.BlockSpec((K, tn), lambda i, j: (0, j))]
    args = [h, w]
    for arr, kind in extra:
        if kind == "col":
            in_specs.append(pl.BlockSpec((1, tn), lambda i, j: (0, j)))
        elif kind == "tile":
            in_specs.append(pl.BlockSpec((1, LANES), lambda i, j: (0, 0)))
        else:
            in_specs.append(pl.BlockSpec((tm, LANES), lambda i, j: (i, 0)))
        args.append(arr)
    return pl.pallas_call(
        functools.partial(_proj_body, epi, len(extra)),
        out_shape=jax.ShapeDtypeStruct((S, N), out_dtype),
        grid=(S // tm, N // tn),
        in_specs=in_specs,
        out_specs=pl.BlockSpec((tm, tn), lambda i, j: (i, j)),
        compiler_params=_params("parallel", "parallel"),
        name=name,
    )(*args)


def _outproj_body(n_a, has_next, *refs):
    a_refs = refs[:n_a]
    w_ref, x_ref = refs[n_a], refs[n_a + 1]
    if has_next:
        g_ref, xo_ref, ho_ref = refs[n_a + 2:n_a + 5]
    else:
        xo_ref = refs[n_a + 2]
    if n_a == 1:
        a = a_refs[0][...]
    else:
        a = a_refs[0][...].astype(F32)
        for r in a_refs[1:]:
            a = a + r[...].astype(F32)
        a = a.astype(BF16)
    xn = x_ref[...] + jnp.dot(a, w_ref[...], preferred_element_type=F32)
    xo_ref[...] = xn
    if has_next:
        ms = jnp.mean(xn * xn, axis=-1, keepdims=True)
        ho_ref[...] = (xn * lax.rsqrt(ms + EPS) * g_ref[...]).astype(BF16)


def outproj(a_list, w, x, g_next=None, tm=256):
    S, D = x.shape
    n_a = len(a_list)
    has_next = g_next is not None
    row = pl.BlockSpec((tm, D), lambda i: (i, 0))
    in_specs = [pl.BlockSpec((tm, MIX_WIDTH), lambda i: (i, 0)) for _ in a_list]
    in_specs += [pl.BlockSpec((MIX_WIDTH, D), lambda i: (0, 0)), row]
    args = list(a_list) + [w, x]
    out_shape = [jax.ShapeDtypeStruct((S, D), F32)]
    out_specs = [row]
    if has_next:
        in_specs.append(pl.BlockSpec((1, D), lambda i: (0, 0)))
        args.append(g_next.reshape(1, D))
        out_shape.append(jax.ShapeDtypeStruct((S, D), BF16))
        out_specs.append(row)
    res = pl.pallas_call(
        functools.partial(_outproj_body, n_a, has_next),
        out_shape=out_shape,
        grid=(S // tm,),
        in_specs=in_specs,
        out_specs=out_specs,
        compiler_params=_params("parallel"),
        name="outproj",
    )(*args)
    return (res[0], res[1]) if has_next else (res[0], None)


def _flash_body(T, n_k, has_qx, has_alibi, has_og, window, *refs):
    pos = 0
    q_ref = refs[pos]; pos += 1
    qx_ref = None
    if has_qx:
        qx_ref = refs[pos]; pos += 1
    k_refs = refs[pos:pos + n_k]; pos += n_k
    v_ref = refs[pos]; pos += 1
    if has_alibi:
        al_ref, sl_ref = refs[pos], refs[pos + 1]; pos += 2
    omul_ref = refs[pos]; pos += 1
    if has_og:
        og_ref = refs[pos]; pos += 1
    o_ref = refs[pos]

    i = pl.program_id(1)
    q = q_ref[...]
    if has_qx:
        q = jnp.concatenate([q, qx_ref[...]], axis=-1)
    rows = lax.broadcasted_iota(jnp.int32, (T, T), 0)
    cols = lax.broadcasted_iota(jnp.int32, (T, T), 1)
    tri = rows - cols

    def tile(j, carry, masked):
        m, l, acc = carry
        start = pl.multiple_of(j * T, T)
        ks = [kr[pl.ds(start, T), :] for kr in k_refs]
        k = ks[0] if n_k == 1 else jnp.concatenate(ks, axis=-1)
        s = lax.dot_general(q, k, (((1,), (1,)), ((), ())), preferred_element_type=F32)
        if has_alibi:
            s = s + (al_ref[...] + sl_ref[...] * ((j - i) * T).astype(F32))
        if masked:
            dist = tri + (i - j) * T
            ok = dist >= 0
            if window is not None:
                ok = ok & (dist < window)
            s = jnp.where(ok, s, NEG_BIG)
        m_new = jnp.maximum(m, jnp.max(s, axis=-1, keepdims=True))
        alpha = jnp.exp(m - m_new)
        p = jnp.exp(s - m_new)
        l = alpha * l + jnp.sum(p, axis=-1, keepdims=True)
        acc = alpha * acc + jnp.dot(p.astype(BF16), v_ref[pl.ds(start, T), :],
                                    preferred_element_type=F32)
        return m_new, l, acc

    init = (jnp.full((T, 1), NEG_BIG, F32), jnp.zeros((T, 1), F32), jnp.zeros((T, LANES), F32))
    carry = tile(i, init, True)
    if window is None:
        lo = 0
        rest_masked = False
    else:
        lo = jnp.maximum(i - (window + T - 1) // T, 0)
        rest_masked = True
    _, l, acc = lax.fori_loop(lo, i, lambda j, c: tile(j, c, rest_masked), carry)
    o = acc / l * omul_ref[...].astype(F32)
    if has_og:
        o = o * og_ref[...]
    o_ref[...] = o.astype(o_ref.dtype)


def flash(q, k_parts, v, omul, *, dq, kv_of, T=256, qx=None, qx_of=None, slopes=None,
          og=None, og_row0=0, window=None, name="flash"):
    S = q.shape[0]
    H = N_HEADS
    nq = S // T
    in_specs = [pl.BlockSpec((T, dq), lambda h, i: (i, h))]
    args = [q]
    if qx is not None:
        in_specs.append(pl.BlockSpec((None, T, LANES), lambda h, i: (qx_of(h), i, 0)))
        args.append(qx)
    for arr, per_head in k_parts:
        if per_head:
            in_specs.append(pl.BlockSpec((S, LANES), lambda h, i: (0, kv_of(h))))
        else:
            in_specs.append(pl.BlockSpec((S, LANES), lambda h, i: (0, 0)))
        args.append(arr)
    in_specs.append(pl.BlockSpec((S, LANES), lambda h, i: (0, kv_of(h))))
    args.append(v)
    if slopes is not None:
        sl = np.asarray(slopes, np.float64)[:, None, None]
        al = (sl * np.arange(T)[None, None, :]).astype(np.float32)
        sl = np.broadcast_to(sl, (H, 1, T)).astype(np.float32)
        in_specs += [pl.BlockSpec((None, 1, T), lambda h, i: (h, 0, 0))] * 2
        args += [jnp.asarray(al), jnp.asarray(sl)]
    in_specs.append(pl.BlockSpec((T, LANES), lambda h, i: (i, h)))
    args.append(omul)
    if og is not None:
        in_specs.append(pl.BlockSpec((None, T, 1), lambda h, i: (og_row0 + h, i, 0)))
        args.append(og)
    body = functools.partial(_flash_body, T, len(k_parts), qx is not None, slopes is not None,
                             og is not None, window)
    return pl.pallas_call(
        body,
        out_shape=jax.ShapeDtypeStruct((S, H * LANES), BF16),
        grid=(H, nq),
        in_specs=in_specs,
        out_specs=pl.BlockSpec((T, LANES), lambda h, i: (i, h)),
        compiler_params=_params("parallel", "arbitrary"),
        name=name,
    )(*args)


def _block_onehot(S, block):
    e = (np.arange(S)[:, None] // block) == np.arange(LANES)[None, :]
    return jnp.asarray(e.astype(np.float32), dtype=BF16)


def _take_top(score, col, n):
    colf = col.astype(F32)
    taken = jnp.zeros(score.shape, jnp.bool_)
    for _ in range(n):
        m = jnp.max(score, axis=-1, keepdims=True)
        idx = jnp.min(jnp.where(score == m, colf, float(LANES)), axis=-1, keepdims=True)
        hit = colf == idx
        taken = taken | (hit & (m > -jnp.inf))
        score = jnp.where(hit, -jnp.inf, score)
    return taken


def _moba_select_body(T, NB, q_ref, k_ref, o_ref, km_ref):
    i = pl.program_id(1)

    @pl.when(i == 0)
    def _():
        km_ref[...] = jnp.zeros_like(km_ref)
        for b in range(NB):
            kb = k_ref[b * MOBA_BLOCK:(b + 1) * MOBA_BLOCK, :].astype(F32)
            km_ref[b:b + 1, :] = jnp.sum(kb, axis=0, keepdims=True) * (1.0 / MOBA_BLOCK)

    q = q_ref[...]
    km = km_ref[...]
    km_hi = km.astype(BF16)
    km_lo = (km - km_hi.astype(F32)).astype(BF16)
    dn = (((1,), (1,)), ((), ()))
    score = (lax.dot_general(q, km_hi, dn, preferred_element_type=F32)
             + lax.dot_general(q, km_lo, dn, preferred_element_type=F32))
    col = lax.broadcasted_iota(jnp.int32, (T, LANES), 1)
    row = lax.broadcasted_iota(jnp.int32, (T, LANES), 0)
    q_blk = (i * T + row) // MOBA_BLOCK
    score = jnp.where(col < q_blk, score, -jnp.inf)
    taken = _take_top(score, col, min(MOBA_TOPK, NB)) | (col == q_blk)
    o_ref[...] = jnp.where(taken, 0.0, NEG_BIG).astype(o_ref.dtype)


def moba_select(q, k, T=512):
    S = q.shape[0]
    NB = S // MOBA_BLOCK
    return pl.pallas_call(
        functools.partial(_moba_select_body, T, NB),
        out_shape=jax.ShapeDtypeStruct((N_HEADS, S, LANES), BF16),
        grid=(N_HEADS, S // T),
        in_specs=[pl.BlockSpec((T, LANES), lambda h, i: (i, h)),
                  pl.BlockSpec((S, LANES), lambda h, i: (0, h))],
        out_specs=pl.BlockSpec((None, T, LANES), lambda h, i: (h, i, 0)),
        scratch_shapes=[pltpu.VMEM((LANES, LANES), F32)],
        compiler_params=_params("parallel", "arbitrary"),
        name="moba_select",
    )(q, k)


def _softplus(z):
    return jnp.maximum(z, 0.0) + jnp.log(1.0 + jnp.exp(-jnp.abs(z)))


def _stick_body(T, q_ref, k_ref, v_ref, omul_ref, o_ref):
    i = pl.program_id(1)
    q = q_ref[...]
    rows = lax.broadcasted_iota(jnp.int32, (T, T), 0)
    cols = lax.broadcasted_iota(jnp.int32, (T, T), 1)
    later = jnp.where(rows > cols, 1.0, 0.0).astype(BF16)
    before = cols < rows

    def tile(j, run, acc, diag):
        start = pl.multiple_of(j * T, T)
        k = k_ref[pl.ds(start, T), :]
        z = lax.dot_general(q, k, (((1,), (1,)), ((), ())), preferred_element_type=F32)
        log_keep = -_softplus(z)
        if diag:
            log_keep = jnp.where(before, log_keep, 0.0)
        hi = log_keep.astype(BF16)
        lo = (log_keep - hi.astype(F32)).astype(BF16)
        between = (jnp.dot(hi, later, preferred_element_type=F32)
                   + jnp.dot(lo, later, preferred_element_type=F32))
        a = jnp.exp(z + log_keep + between + run)
        if diag:
            a = jnp.where(before, a, 0.0)
        acc = acc + jnp.dot(a.astype(BF16), v_ref[pl.ds(start, T), :], preferred_element_type=F32)
        run = run + jnp.sum(log_keep, axis=-1, keepdims=True)
        return run, acc

    run, acc = tile(i, jnp.zeros((T, 1), F32), jnp.zeros((T, LANES), F32), True)

    def cond(st):
        j, run, _ = st
        return jnp.logical_and(j >= 0, jnp.max(run) > SB_LOG_CUTOFF)

    def step(st):
        j, run, acc = st
        run, acc = tile(j, run, acc, False)
        return j - 1, run, acc

    _, _, acc = lax.while_loop(cond, step, (i - 1, run, acc))
    o_ref[...] = (acc * omul_ref[...].astype(F32)).astype(o_ref.dtype)


def stick_attention(q, k, v, omul, T=256):
    S = q.shape[0]
    qspec = pl.BlockSpec((T, LANES), lambda h, i: (i, h))
    kspec = pl.BlockSpec((S, LANES), lambda h, i: (0, h))
    return pl.pallas_call(
        functools.partial(_stick_body, T),
        out_shape=jax.ShapeDtypeStruct((S, MIX_WIDTH), BF16),
        grid=(N_HEADS, S // T),
        in_specs=[qspec, kspec, kspec, qspec],
        out_specs=qspec,
        compiler_params=_params("parallel", "arbitrary"),
        name="stick_attention",
    )(q, k, v, omul)


def _nsa_compress_body(norm, x_ref, w_ref, pos_ref, g_ref, o_ref):
    half = NSA_CMP_STRIDE * HEAD_DIM
    x = x_ref[...]
    first = jnp.dot(x, w_ref[:half, :], preferred_element_type=F32)
    second = jnp.dot(x, w_ref[half:, :], preferred_element_type=F32)
    bias = jnp.dot(pos_ref[...], w_ref[...], preferred_element_type=F32)
    n = x.shape[0]
    y = first + pltpu.roll(second, n - 1, axis=0) + bias[0:1, :]
    if norm:
        y = _head_rms(y, g_ref[...], HEAD_DIM)
    o_ref[...] = y.astype(o_ref.dtype)


def nsa_compress(x, w, pos, gain, norm):
    G, n, width = x.shape
    pos_flat = jnp.broadcast_to(pos.reshape(1, -1), (16, pos.size)).astype(BF16)
    return pl.pallas_call(
        functools.partial(_nsa_compress_body, norm),
        out_shape=jax.ShapeDtypeStruct((G, n, HEAD_DIM), BF16),
        grid=(G,),
        in_specs=[pl.BlockSpec((None, n, width), lambda g: (g, 0, 0)),
                  pl.BlockSpec(w.shape, lambda g: (0, 0)),
                  pl.BlockSpec((16, pos.size), lambda g: (0, 0)),
                  pl.BlockSpec((1, HEAD_DIM), lambda g: (0, 0))],
        out_specs=pl.BlockSpec((None, n, HEAD_DIM), lambda g: (g, 0, 0)),
        compiler_params=_params("parallel"),
        name="nsa_compress",
    )(x, w.astype(BF16), pos_flat, gain.reshape(1, HEAD_DIM))


def _nsa_cmp_body(T, n_cmp, n_sel, q_ref, kc_ref, vc_ref, c2s_ref, sl_ref, omul_ref, og_ref,
                  o_ref, sel_ref):
    i = pl.program_id(1)
    NC = kc_ref.shape[0]
    kc = kc_ref[...]
    vc = vc_ref[...]
    c2s = c2s_ref[...]
    row = lax.broadcasted_iota(jnp.int32, (T, NC), 0)
    col = lax.broadcasted_iota(jnp.int32, (T, NC), 1)
    dist = i * T + row - (col * NSA_CMP_STRIDE + (NSA_CMP_BLOCK - 1))
    ok = (dist >= 0) & (col < n_cmp)
    dist_f = dist.astype(F32)
    imp = jnp.zeros((T, LANES), F32)
    outs = []
    for p in range(NSA_HPG):
        qp = q_ref[:, p * LANES:(p + 1) * LANES]
        s = lax.dot_general(qp, kc, (((1,), (1,)), ((), ())), preferred_element_type=F32)
        s = jnp.where(ok, s - sl_ref[p] * dist_f, NEG_BIG)
        m = jnp.max(s, axis=-1, keepdims=True)
        e = jnp.where(ok, jnp.exp(s - m), 0.0)
        l = jnp.sum(e, axis=-1, keepdims=True)
        prob = e / jnp.where(l > 0.0, l, 1.0)
        hi = prob.astype(BF16)
        lo = (prob - hi.astype(F32)).astype(BF16)
        o = jnp.dot(hi, vc, preferred_element_type=F32)
        imp = imp + (jnp.dot(hi, c2s, preferred_element_type=F32)
                     + jnp.dot(lo, c2s, preferred_element_type=F32))
        o = o * og_ref[p] * omul_ref[:, p * LANES:(p + 1) * LANES].astype(F32)
        outs.append(o.astype(o_ref.dtype))
    o_ref[...] = jnp.concatenate(outs, axis=-1)

    blk = lax.broadcasted_iota(jnp.int32, (T, LANES), 1)
    cur = (i * T + lax.broadcasted_iota(jnp.int32, (T, LANES), 0)) // NSA_SEL_BLOCK
    forced = (blk == 0) | (blk == cur) | (blk == cur - 1)
    imp = jnp.where((blk > cur) | (blk >= n_sel), -jnp.inf, jnp.where(forced, jnp.inf, imp))
    taken = _take_top(imp, blk, min(NSA_SEL_N, n_sel))
    sel_ref[...] = jnp.where(taken, 0.0, NEG_BIG).astype(sel_ref.dtype)


def nsa_compressed(q, k_cmp, v_cmp, omul, og, T=256):
    S = q.shape[0]
    G, NC, _ = k_cmp.shape
    n_cmp = (S - NSA_CMP_BLOCK) // NSA_CMP_STRIDE + 1
    n_sel = S // NSA_SEL_BLOCK
    assert n_sel <= LANES and NC >= n_cmp
    c_start = np.arange(NC)[:, None] * NSA_CMP_STRIDE
    s_start = np.arange(LANES)[None, :] * NSA_SEL_BLOCK
    c2s = ((c_start < s_start + NSA_SEL_BLOCK) & (c_start + NSA_CMP_BLOCK > s_start)
           & (np.arange(NC)[:, None] < n_cmp) & (np.arange(LANES)[None, :] < n_sel))
    slopes = np.broadcast_to(_alibi_slopes()[:, None, None], (N_HEADS, 1, NC)).astype(np.float32)
    gw = NSA_HPG * LANES
    grp = pl.BlockSpec((T, gw), lambda g, i: (i, g))
    cmp_spec = pl.BlockSpec((None, NC, HEAD_DIM), lambda g, i: (g, 0, 0))
    return pl.pallas_call(
        functools.partial(_nsa_cmp_body, T, n_cmp, n_sel),
        out_shape=[jax.ShapeDtypeStruct((S, MIX_WIDTH), BF16),
                   jax.ShapeDtypeStruct((G, S, LANES), BF16)],
        grid=(G, S // T),
        in_specs=[grp, cmp_spec, cmp_spec,
                  pl.BlockSpec((NC, LANES), lambda g, i: (0, 0)),
                  pl.BlockSpec((NSA_HPG, 1, NC), lambda g, i: (g, 0, 0)),
                  grp,
                  pl.BlockSpec((NSA_HPG, T, 1), lambda g, i: (g, i, 0))],
        out_specs=[grp, pl.BlockSpec((None, T, LANES), lambda g, i: (g, i, 0))],
        compiler_params=_params("parallel", "parallel"),
        name="nsa_compressed",
    )(q, k_cmp, v_cmp, jnp.asarray(c2s.astype(np.float32), dtype=BF16), jnp.asarray(slopes),
      omul, og)


def _row(v):
    return v.reshape(1, -1).astype(F32)


def _rope_tables(S):
    inv_freq = ROPE_THETA ** (-jnp.arange(0, MLA_ROPE, 2, dtype=F32) / MLA_ROPE)
    ang = jnp.arange(S, dtype=F32)[:, None] * inv_freq[None, :]
    cos, sin = jnp.cos(ang), jnp.sin(ang)
    z = jnp.zeros_like(cos)
    return (jnp.concatenate([cos, z, cos, z], axis=-1),
            jnp.concatenate([-sin, z, sin, z], axis=-1))


def _rope_lanes(v):
    half = MLA_ROPE // 2
    z = jnp.zeros(v.shape[:-1] + (half,), v.dtype)
    return jnp.concatenate([v[..., :half], z, v[..., half:], z], axis=-1)


def mla_mixer(h, w_in, q_lat_norm, w_q_up, kv_lat_norm, w_kv_up, q_norm, k_norm):
    S = h.shape[0]
    c1 = MLA_Q_RANK
    c2 = c1 + MLA_KV_RANK
    c3 = c2 + MLA_ROPE
    scale = MLA_QK_DIM ** -0.5
    cos, sin = _rope_tables(S)
    gq_n, gq_r = _row(q_norm[:MLA_NOPE]), _row(_rope_lanes(q_norm[MLA_NOPE:]))
    gk_n, gk_r = _row(k_norm[:MLA_NOPE]), _row(_rope_lanes(k_norm[MLA_NOPE:]))

    q_lat = proj(h, w_in[:, :c1].astype(BF16), _epi_rownorm, [(_row(q_lat_norm), "col")], name="mla_qlat")
    kv_lat = proj(h, w_in[:, c1:c2].astype(BF16), _epi_rownorm, [(_row(kv_lat_norm), "col")], name="mla_kvlat")
    k_rope = proj(h, _rope_lanes(w_in[:, c2:c3]).astype(BF16), _epi_mla_krope,
                  [(gk_r, "tile"), (cos, "row"), (sin, "row")], name="mla_krope")
    gate = proj(h, w_in[:, c3:].astype(BF16), _epi_silu, tn=1024, name="mla_gate")

    wq = w_q_up.reshape(MLA_Q_RANK, N_HEADS, MLA_QK_DIM)
    wq = jnp.concatenate([wq[..., :MLA_NOPE], _rope_lanes(wq[..., MLA_NOPE:])], axis=-1)
    q = proj(q_lat, wq.reshape(MLA_Q_RANK, N_HEADS * 2 * LANES).astype(BF16),
             functools.partial(_epi_mla_q, scale=scale),
             [(gq_n, "tile"), (gq_r, "tile"), (cos, "row"), (sin, "row")], tn=1024, name="mla_q")
    wkv = w_kv_up.reshape(MLA_KV_RANK, N_HEADS, MLA_NOPE + HEAD_DIM)
    k_nope = proj(kv_lat, wkv[..., :MLA_NOPE].reshape(MLA_KV_RANK, -1).astype(BF16),
                  functools.partial(_epi_headnorm, scale=1.0), [(gk_n, "tile")], tn=1024, name="mla_knope")
    v = proj(kv_lat, wkv[..., MLA_NOPE:].reshape(MLA_KV_RANK, -1).astype(BF16),
             functools.partial(_epi_none, scale=1.0), tn=1024, name="mla_v")
    return [flash(q, [(k_nope, True), (k_rope, False)], v, gate, dq=2 * LANES, kv_of=lambda hh: hh,
                  name="mla_attention")]


def moba_mixer(h, w_in, q_norm, k_norm):
    S = h.shape[0]
    W = MIX_WIDTH
    scale = HEAD_DIM ** -0.5
    q = proj(h, w_in[:, :W].astype(BF16), functools.partial(_epi_headnorm, scale=scale),
             [(_row(q_norm), "tile")], tn=1024, name="moba_q")
    k = proj(h, w_in[:, W:2 * W].astype(BF16), functools.partial(_epi_headnorm, scale=1.0),
             [(_row(k_norm), "tile")], tn=1024, name="moba_k")
    v = proj(h, w_in[:, 2 * W:3 * W].astype(BF16), functools.partial(_epi_none, scale=1.0), tn=1024, name="moba_v")
    gate = proj(h, w_in[:, 3 * W:].astype(BF16), _epi_silu, tn=1024, name="moba_gate")
    bias = moba_select(q, k)
    return [flash(q, [(k, True), (_block_onehot(S, MOBA_BLOCK), False)], v, gate, dq=LANES,
                  kv_of=lambda hh: hh, qx=bias, qx_of=lambda hh: hh, slopes=_alibi_slopes(),
                  name="moba_attention")]


def stick_mixer(h, w_in):
    W = MIX_WIDTH
    scale = HEAD_DIM ** -0.5
    q = proj(h, w_in[:, :W].astype(BF16), functools.partial(_epi_none, scale=scale), tn=1024, name="sb_q")
    k = proj(h, w_in[:, W:2 * W].astype(BF16), functools.partial(_epi_none, scale=1.0), tn=1024, name="sb_k")
    v = proj(h, w_in[:, 2 * W:3 * W].astype(BF16), functools.partial(_epi_none, scale=1.0), tn=1024, name="sb_v")
    gate = proj(h, w_in[:, 3 * W:].astype(BF16), _epi_silu, tn=1024, name="sb_gate")
    return [stick_attention(q, k, v, gate)]


def nsa_mixer(h, w_in, q_norm, k_norm, w_cmp_k, w_cmp_v, cmp_pos):
    S = h.shape[0]
    G, P = NSA_GROUPS, NSA_HPG
    W, KV = MIX_WIDTH, NSA_KV_WIDTH
    half = KV // 2
    scale = HEAD_DIM ** -0.5
    cuts = np.cumsum([W, KV, KV, KV, 3 * N_HEADS]).tolist()
    plain = functools.partial(_epi_none, scale=1.0)

    def knorm(b):
        return functools.partial(_epi_headnorm, scale=1.0), [(_row(k_norm[b]), "tile")]

    q = proj(h, w_in[:, :W].astype(BF16), functools.partial(_epi_headnorm, scale=scale),
             [(_row(q_norm), "tile")], tn=1024, name="nsa_q")
    kv_c = proj(h, w_in[:, cuts[0]:cuts[1]].astype(BF16), plain, name="nsa_kvc")
    ks = proj(h, w_in[:, cuts[1]:cuts[1] + half].astype(BF16), *knorm(1), name="nsa_ks")
    vs = proj(h, w_in[:, cuts[1] + half:cuts[2]].astype(BF16), plain, name="nsa_vs")
    kw = proj(h, w_in[:, cuts[2]:cuts[2] + half].astype(BF16), *knorm(2), name="nsa_kw")
    vw = proj(h, w_in[:, cuts[2] + half:cuts[3]].astype(BF16), plain, name="nsa_vw")
    wg = w_in[:, cuts[3]:cuts[4]].reshape(-1, N_HEADS, 3).transpose(0, 2, 1).reshape(-1, 3 * N_HEADS)
    wg = jnp.pad(wg, ((0, 0), (0, LANES - 3 * N_HEADS)))
    og = proj(h, wg.astype(BF16), _epi_sigmoid, out_dtype=F32, name="nsa_branch_gates")
    og = og[:, :3 * N_HEADS].T[:, :, None]
    gate = proj(h, w_in[:, cuts[4]:].astype(BF16), _epi_silu, tn=1024, name="nsa_gate")

    def chunks(t):
        return t.reshape(S // NSA_CMP_STRIDE, NSA_CMP_STRIDE, G, HEAD_DIM).transpose(2, 0, 1, 3).reshape(
            G, S // NSA_CMP_STRIDE, NSA_CMP_STRIDE * HEAD_DIM)

    k_cmp = nsa_compress(chunks(kv_c[:, :half]), w_cmp_k, cmp_pos, k_norm[0], True)
    v_cmp = nsa_compress(chunks(kv_c[:, half:]), w_cmp_v, cmp_pos, k_norm[0], False)
    o_cmp, bias = nsa_compressed(q, k_cmp, v_cmp, gate, og)
    slopes = _alibi_slopes()
    o_slc = flash(q, [(ks, True), (_block_onehot(S, NSA_SEL_BLOCK), False)], vs, gate, dq=LANES,
                  kv_of=lambda hh: hh // P, qx=bias, qx_of=lambda hh: hh // P, slopes=slopes,
                  og=og, og_row0=N_HEADS, name="nsa_selected")
    o_win = flash(q, [(kw, True)], vw, gate, dq=LANES, kv_of=lambda hh: hh // P, slopes=slopes,
                  og=og, og_row0=2 * N_HEADS, window=NSA_WINDOW, name="nsa_window")
    return [o_cmp, o_slc, o_win]


def kernel(x, norm_a, w_in_a, q_lat_norm_a, w_q_up_a, kv_lat_norm_a, w_kv_up_a, q_norm_a, k_norm_a, w_out_a,
           norm_b, w_in_b, q_norm_b, k_norm_b, w_out_b,
           norm_c, w_in_c, w_out_c,
           norm_d, w_in_d, q_norm_d, k_norm_d, w_cmp_k_d, w_cmp_v_d, cmp_pos_d, w_out_d):
    B, S, D = x.shape
    norms = (norm_a, norm_b, norm_c, norm_d)
    w_outs = (w_out_a, w_out_b, w_out_c, w_out_d)
    depth = norm_a.shape[0] + norm_b.shape[0] + norm_c.shape[0] + norm_d.shape[0]
    outs = []
    for b in range(B):
        xb = x[b]
        h = rmsnorm_rows(xb, norms[0][0])
        for layer in range(depth):
            m, j = layer % 4, layer // 4
            if m == 0:
                a = mla_mixer(h, w_in_a[j], q_lat_norm_a[j], w_q_up_a[j], kv_lat_norm_a[j], w_kv_up_a[j],
                              q_norm_a[j], k_norm_a[j])
            elif m == 1:
                a = moba_mixer(h, w_in_b[j], q_norm_b[j], k_norm_b[j])
            elif m == 2:
                a = stick_mixer(h, w_in_c[j])
            else:
                a = nsa_mixer(h, w_in_d[j], q_norm_d[j], k_norm_d[j], w_cmp_k_d[j], w_cmp_v_d[j], cmp_pos_d[j])
            nxt = layer + 1
            g_next = norms[nxt % 4][nxt // 4] if nxt < depth else None
            xb, h = outproj(a, w_outs[m][j].astype(BF16), xb, g_next)
        outs.append(xb)
    return jnp.stack(outs, axis=0)
```

```python
import functools

import numpy as np
import jax
import jax.numpy as jnp
from jax import lax
from jax.experimental import pallas as pl
from jax.experimental.pallas import tpu as pltpu

N_HEADS = 16
HEAD_DIM = 128
MIX_WIDTH = N_HEADS * HEAD_DIM
EPS = 1e-6
NEG_BIG = -1e30
LANES = 128

MLA_Q_RANK = 512
MLA_KV_RANK = 256
MLA_NOPE = 128
MLA_ROPE = 64
MLA_QK_DIM = MLA_NOPE + MLA_ROPE
ROPE_THETA = 10000.0

MOBA_BLOCK = 256
MOBA_TOPK = 3

NSA_GROUPS = 4
NSA_HPG = N_HEADS // NSA_GROUPS
NSA_CMP_BLOCK = 32
NSA_CMP_STRIDE = 16
NSA_SEL_BLOCK = 64
NSA_SEL_N = 16
NSA_WINDOW = 512
NSA_KV_WIDTH = 2 * NSA_GROUPS * HEAD_DIM

SB_LOG_CUTOFF = -110.0

VMEM_LIMIT = 56 * 1024 * 1024

LOG2E = 1.4426950408889634

FLASH_T = 1024
WINDOW_T = 512

BF16 = jnp.bfloat16
F32 = jnp.float32


def _params(*sem):
    return pltpu.CompilerParams(dimension_semantics=sem, vmem_limit_bytes=VMEM_LIMIT)


def _alibi_slopes():
    return 2.0 ** (-8.0 * np.arange(1, N_HEADS + 1) / N_HEADS)


def _rmsnorm_body(x_ref, g_ref, o_ref):
    x = x_ref[...]
    ms = jnp.mean(x * x, axis=-1, keepdims=True)
    o_ref[...] = (x * lax.rsqrt(ms + EPS) * g_ref[...]).astype(o_ref.dtype)


def rmsnorm_rows(x, g, tm=512):
    S, D = x.shape
    return pl.pallas_call(
        _rmsnorm_body,
        out_shape=jax.ShapeDtypeStruct((S, D), BF16),
        grid=(S // tm,),
        in_specs=[pl.BlockSpec((tm, D), lambda i: (i, 0)),
                  pl.BlockSpec((1, D), lambda i: (0, 0))],
        out_specs=pl.BlockSpec((tm, D), lambda i: (i, 0)),
        compiler_params=_params("parallel"),
        name="rmsnorm_rows",
    )(x, g.reshape(1, D))


def _head_rms(y, gain, n_real):
    ms = jnp.sum(y * y, axis=-1, keepdims=True) * (1.0 / n_real)
    return y * lax.rsqrt(ms + EPS) * gain


def _epi_none(acc, scale):
    return acc * scale if scale != 1.0 else acc


def _epi_silu(acc):
    return acc / (1.0 + jnp.exp(-acc))


def _epi_sigmoid(acc):
    return 1.0 / (1.0 + jnp.exp(-acc))


def _epi_rownorm(acc, g_ref):
    return _head_rms(acc, g_ref[...], acc.shape[-1])


def _epi_headnorm(acc, g_ref, scale):
    g = g_ref[...] * scale
    parts = [_head_rms(acc[:, c:c + LANES], g, LANES) for c in range(0, acc.shape[-1], LANES)]
    return parts[0] if len(parts) == 1 else jnp.concatenate(parts, axis=-1)


def _rope_slab(y, g_ref, cos_ref, sin_ref, scale):
    y = _head_rms(y, g_ref[...], MLA_ROPE)
    return (y * cos_ref[...] + pltpu.roll(y, 64, axis=1) * sin_ref[...]) * scale


def _epi_mla_q(acc, gn_ref, gr_ref, cos_ref, sin_ref, scale):
    parts = []
    gn = gn_ref[...] * scale
    for c in range(0, acc.shape[-1], 2 * LANES):
        parts.append(_head_rms(acc[:, c:c + LANES], gn, MLA_NOPE))
        parts.append(_rope_slab(acc[:, c + LANES:c + 2 * LANES], gr_ref, cos_ref, sin_ref, scale))
    return jnp.concatenate(parts, axis=-1)


def _epi_mla_krope(acc, gr_ref, cos_ref, sin_ref):
    return _rope_slab(acc, gr_ref, cos_ref, sin_ref, 1.0)


def _proj_body(epi, n_extra, h_ref, w_ref, *refs):
    extra, o_ref = refs[:n_extra], refs[n_extra]
    acc = jnp.dot(h_ref[...], w_ref[...], preferred_element_type=F32)
    o_ref[...] = epi(acc, *extra).astype(o_ref.dtype)


def proj(h, w, epi, extra=(), *, tm=512, tn=None, out_dtype=BF16, name="proj"):
    S, K = h.shape
    N = w.shape[1]
    tn = N if tn is None else tn
    in_specs = [pl.BlockSpec((tm, K), lambda i, j: (i, 0)),
                pl.BlockSpec((K, tn), lambda i, j: (0, j))]
    args = [h, w]
    for arr, kind in extra:
        if kind == "col":
            in_specs.append(pl.BlockSpec((1, tn), lambda i, j: (0, j)))
        elif kind == "tile":
            in_specs.append(pl.BlockSpec((1, LANES), lambda i, j: (0, 0)))
        else:
            in_specs.append(pl.BlockSpec((tm, LANES), lambda i, j: (i, 0)))
        args.append(arr)
    return pl.pallas_call(
        functools.partial(_proj_body, epi, len(extra)),
        out_shape=jax.ShapeDtypeStruct((S, N), out_dtype),
        grid=(S // tm, N // tn),
        in_specs=in_specs,
        out_specs=pl.BlockSpec((tm, tn), lambda i, j: (i, j)),
        compiler_params=_params("parallel", "parallel"),
        name=name,
    )(*args)


def _outproj_body(n_a, has_next, *refs):
    a_refs = refs[:n_a]
    w_ref, x_ref = refs[n_a], refs[n_a + 1]
    if has_next:
        g_ref, xo_ref, ho_ref = refs[n_a + 2:n_a + 5]
    else:
        xo_ref = refs[n_a + 2]
    if n_a == 1:
        a = a_refs[0][...]
    else:
        a = a_refs[0][...].astype(F32)
        for r in a_refs[1:]:
            a = a + r[...].astype(F32)
        a = a.astype(BF16)
    xn = x_ref[...] + jnp.dot(a, w_ref[...], preferred_element_type=F32)
    xo_ref[...] = xn
    if has_next:
        ms = jnp.mean(xn * xn, axis=-1, keepdims=True)
        ho_ref[...] = (xn * lax.rsqrt(ms + EPS) * g_ref[...]).astype(BF16)


def outproj(a_list, w, x, g_next=None, tm=256):
    S, D = x.shape
    n_a = len(a_list)
    has_next = g_next is not None
    row = pl.BlockSpec((tm, D), lambda i: (i, 0))
    in_specs = [pl.BlockSpec((tm, MIX_WIDTH), lambda i: (i, 0)) for _ in a_list]
    in_specs += [pl.BlockSpec((MIX_WIDTH, D), lambda i: (0, 0)), row]
    args = list(a_list) + [w, x]
    out_shape = [jax.ShapeDtypeStruct((S, D), F32)]
    out_specs = [row]
    if has_next:
        in_specs.append(pl.BlockSpec((1, D), lambda i: (0, 0)))
        args.append(g_next.reshape(1, D))
        out_shape.append(jax.ShapeDtypeStruct((S, D), BF16))
        out_specs.append(row)
    res = pl.pallas_call(
        functools.partial(_outproj_body, n_a, has_next),
        out_shape=out_shape,
        grid=(S // tm,),
        in_specs=in_specs,
        out_specs=out_specs,
        compiler_params=_params("parallel"),
        name="outproj",
    )(*args)
    return (res[0], res[1]) if has_next else (res[0], None)


def _flash_body(T, n_k, has_qx, has_alibi, has_og, window, *refs):
    pos = 0
    q_ref = refs[pos]; pos += 1
    qx_ref = None
    if has_qx:
        qx_ref = refs[pos]; pos += 1
    k_refs = refs[pos:pos + n_k]; pos += n_k
    v_ref = refs[pos]; pos += 1
    if has_alibi:
        al_ref, sl_ref = refs[pos], refs[pos + 1]; pos += 2
    omul_ref = refs[pos]; pos += 1
    if has_og:
        og_ref = refs[pos]; pos += 1
    o_ref = refs[pos]

    i = pl.program_id(1)
    q = q_ref[...]
    if has_qx:
        q = jnp.concatenate([q, qx_ref[...]], axis=-1)
    rows = lax.broadcasted_iota(jnp.int32, (T, T), 0)
    cols = lax.broadcasted_iota(jnp.int32, (T, T), 1)
    tri = rows - cols

    def tile(j, carry, masked):
        m, l, acc = carry
        start = pl.multiple_of(j * T, T)
        ks = [kr[pl.ds(start, T), :] for kr in k_refs]
        k = ks[0] if n_k == 1 else jnp.concatenate(ks, axis=-1)
        s = lax.dot_general(q, k, (((1,), (1,)), ((), ())), preferred_element_type=F32)
        if has_alibi:
            s = s + (al_ref[...] + sl_ref[...] * ((j - i) * T).astype(F32))
        if masked:
            dist = tri + (i - j) * T
            ok = dist >= 0
            if window is not None:
                ok = ok & (dist < window)
            s = jnp.where(ok, s, NEG_BIG)
        m_new = jnp.maximum(m, jnp.max(s, axis=-1, keepdims=True))
        alpha = jnp.exp2(m - m_new)
        p = jnp.exp2(s - m_new)
        l = alpha * l + jnp.sum(p, axis=-1, keepdims=True)
        acc = alpha * acc + jnp.dot(p.astype(BF16), v_ref[pl.ds(start, T), :],
                                    preferred_element_type=F32)
        return m_new, l, acc

    init = (jnp.full((T, 1), NEG_BIG, F32), jnp.zeros((T, 1), F32), jnp.zeros((T, LANES), F32))
    carry = tile(i, init, True)
    if window is None:
        lo = 0
        rest_masked = False
    else:
        lo = jnp.maximum(i - (window + T - 1) // T, 0)
        rest_masked = True
    _, l, acc = lax.fori_loop(lo, i, lambda j, c: tile(j, c, rest_masked), carry)
    o = acc / l * omul_ref[...].astype(F32)
    if has_og:
        o = o * og_ref[...]
    o_ref[...] = o.astype(o_ref.dtype)


def flash(q, k_parts, v, omul, *, dq, kv_of, T, qx=None, qx_of=None, slopes=None,
          og=None, og_row0=0, window=None, name="flash"):
    S = q.shape[0]
    H = N_HEADS
    nq = S // T
    in_specs = [pl.BlockSpec((T, dq), lambda h, i: (i, h))]
    args = [q]
    if qx is not None:
        in_specs.append(pl.BlockSpec((None, T, LANES), lambda h, i: (qx_of(h), i, 0)))
        args.append(qx)
    for arr, per_head in k_parts:
        if per_head:
            in_specs.append(pl.BlockSpec((S, LANES), lambda h, i: (0, kv_of(h))))
        else:
            in_specs.append(pl.BlockSpec((S, LANES), lambda h, i: (0, 0)))
        args.append(arr)
    in_specs.append(pl.BlockSpec((S, LANES), lambda h, i: (0, kv_of(h))))
    args.append(v)
    if slopes is not None:
        sl = np.asarray(slopes, np.float64)[:, None, None] * LOG2E
        al = (sl * np.arange(T)[None, None, :]).astype(np.float32)
        sl = np.broadcast_to(sl, (H, 1, T)).astype(np.float32)
        in_specs += [pl.BlockSpec((None, 1, T), lambda h, i: (h, 0, 0))] * 2
        args += [jnp.asarray(al), jnp.asarray(sl)]
    in_specs.append(pl.BlockSpec((T, LANES), lambda h, i: (i, h)))
    args.append(omul)
    if og is not None:
        in_specs.append(pl.BlockSpec((None, T, 1), lambda h, i: (og_row0 + h, i, 0)))
        args.append(og)
    body = functools.partial(_flash_body, T, len(k_parts), qx is not None, slopes is not None,
                             og is not None, window)
    return pl.pallas_call(
        body,
        out_shape=jax.ShapeDtypeStruct((S, H * LANES), BF16),
        grid=(H, nq),
        in_specs=in_specs,
        out_specs=pl.BlockSpec((T, LANES), lambda h, i: (i, h)),
        compiler_params=_params("parallel", "arbitrary"),
        name=name,
    )(*args)


def _block_onehot(S, block):
    e = (np.arange(S)[:, None] // block) == np.arange(LANES)[None, :]
    return jnp.asarray(e.astype(np.float32), dtype=BF16)


def _take_top(score, col, n):
    colf = col.astype(F32)
    left = score
    for _ in range(n):
        m = jnp.max(left, axis=-1, keepdims=True)
        idx = jnp.min(jnp.where(left == m, colf, float(LANES)), axis=-1, keepdims=True)
        left = jnp.where(colf == idx, -jnp.inf, left)
    return (left == -jnp.inf) & (score > -jnp.inf)


def _moba_select_body(T, NB, q_ref, k_ref, o_ref, km_ref):
    i = pl.program_id(1)

    @pl.when(i == 0)
    def _():
        km_ref[...] = jnp.zeros_like(km_ref)
        for b in range(NB):
            kb = k_ref[b * MOBA_BLOCK:(b + 1) * MOBA_BLOCK, :].astype(F32)
            km_ref[b:b + 1, :] = jnp.sum(kb, axis=0, keepdims=True) * (1.0 / MOBA_BLOCK)

    q = q_ref[...]
    km = km_ref[...]
    km_hi = km.astype(BF16)
    km_lo = (km - km_hi.astype(F32)).astype(BF16)
    dn = (((1,), (1,)), ((), ()))
    score = (lax.dot_general(q, km_hi, dn, preferred_element_type=F32)
             + lax.dot_general(q, km_lo, dn, preferred_element_type=F32))
    col = lax.broadcasted_iota(jnp.int32, (T, LANES), 1)
    row = lax.broadcasted_iota(jnp.int32, (T, LANES), 0)
    q_blk = (i * T + row) // MOBA_BLOCK
    score = jnp.where(col < q_blk, score, -jnp.inf)
    taken = _take_top(score, col, min(MOBA_TOPK, NB)) | (col == q_blk)
    o_ref[...] = jnp.where(taken, 0.0, NEG_BIG).astype(o_ref.dtype)


def moba_select(q, k, T=512):
    S = q.shape[0]
    NB = S // MOBA_BLOCK
    return pl.pallas_call(
        functools.partial(_moba_select_body, T, NB),
        out_shape=jax.ShapeDtypeStruct((N_HEADS, S, LANES), BF16),
        grid=(N_HEADS, S // T),
        in_specs=[pl.BlockSpec((T, LANES), lambda h, i: (i, h)),
                  pl.BlockSpec((S, LANES), lambda h, i: (0, h))],
        out_specs=pl.BlockSpec((None, T, LANES), lambda h, i: (h, i, 0)),
        scratch_shapes=[pltpu.VMEM((LANES, LANES), F32)],
        compiler_params=_params("parallel", "arbitrary"),
        name="moba_select",
    )(q, k)


def _softplus(z):
    return jnp.maximum(z, 0.0) + jnp.log(1.0 + jnp.exp(-jnp.abs(z)))


def _stick_body(T, q_ref, k_ref, v_ref, omul_ref, o_ref):
    i = pl.program_id(1)
    q = q_ref[...]
    rows = lax.broadcasted_iota(jnp.int32, (T, T), 0)
    cols = lax.broadcasted_iota(jnp.int32, (T, T), 1)
    later = jnp.where(rows > cols, 1.0, 0.0).astype(BF16)
    before = cols < rows

    def tile(j, run, acc, diag):
        start = pl.multiple_of(j * T, T)
        k = k_ref[pl.ds(start, T), :]
        z = lax.dot_general(q, k, (((1,), (1,)), ((), ())), preferred_element_type=F32)
        log_keep = -_softplus(z)
        if diag:
            log_keep = jnp.where(before, log_keep, 0.0)
        hi = log_keep.astype(BF16)
        lo = (log_keep - hi.astype(F32)).astype(BF16)
        between = (jnp.dot(hi, later, preferred_element_type=F32)
                   + jnp.dot(lo, later, preferred_element_type=F32))
        a = jnp.exp(z + log_keep + between + run)
        if diag:
            a = jnp.where(before, a, 0.0)
        acc = acc + jnp.dot(a.astype(BF16), v_ref[pl.ds(start, T), :], preferred_element_type=F32)
        run = run + jnp.sum(log_keep, axis=-1, keepdims=True)
        return run, acc

    run, acc = tile(i, jnp.zeros((T, 1), F32), jnp.zeros((T, LANES), F32), True)

    def cond(st):
        j, run, _ = st
        return jnp.logical_and(j >= 0, jnp.max(run) > SB_LOG_CUTOFF)

    def step(st):
        j, run, acc = st
        run, acc = tile(j, run, acc, False)
        return j - 1, run, acc

    _, _, acc = lax.while_loop(cond, step, (i - 1, run, acc))
    o_ref[...] = (acc * omul_ref[...].astype(F32)).astype(o_ref.dtype)


def stick_attention(q, k, v, omul, T=256):
    S = q.shape[0]
    qspec = pl.BlockSpec((T, LANES), lambda h, i: (i, h))
    kspec = pl.BlockSpec((S, LANES), lambda h, i: (0, h))
    return pl.pallas_call(
        functools.partial(_stick_body, T),
        out_shape=jax.ShapeDtypeStruct((S, MIX_WIDTH), BF16),
        grid=(N_HEADS, S // T),
        in_specs=[qspec, kspec, kspec, qspec],
        out_specs=qspec,
        compiler_params=_params("parallel", "arbitrary"),
        name="stick_attention",
    )(q, k, v, omul)


def _nsa_compress_body(norm, x_ref, w_ref, pos_ref, g_ref, o_ref):
    half = NSA_CMP_STRIDE * HEAD_DIM
    x = x_ref[...]
    first = jnp.dot(x, w_ref[:half, :], preferred_element_type=F32)
    second = jnp.dot(x, w_ref[half:, :], preferred_element_type=F32)
    bias = jnp.dot(pos_ref[...], w_ref[...], preferred_element_type=F32)
    n = x.shape[0]
    y = first + pltpu.roll(second, n - 1, axis=0) + bias[0:1, :]
    if norm:
        y = _head_rms(y, g_ref[...], HEAD_DIM)
    o_ref[...] = y.astype(o_ref.dtype)


def nsa_compress(x, w, pos, gain, norm):
    G, n, width = x.shape
    pos_flat = jnp.broadcast_to(pos.reshape(1, -1), (16, pos.size)).astype(BF16)
    return pl.pallas_call(
        functools.partial(_nsa_compress_body, norm),
        out_shape=jax.ShapeDtypeStruct((G, n, HEAD_DIM), BF16),
        grid=(G,),
        in_specs=[pl.BlockSpec((None, n, width), lambda g: (g, 0, 0)),
                  pl.BlockSpec(w.shape, lambda g: (0, 0)),
                  pl.BlockSpec((16, pos.size), lambda g: (0, 0)),
                  pl.BlockSpec((1, HEAD_DIM), lambda g: (0, 0))],
        out_specs=pl.BlockSpec((None, n, HEAD_DIM), lambda g: (g, 0, 0)),
        compiler_params=_params("parallel"),
        name="nsa_compress",
    )(x, w.astype(BF16), pos_flat, gain.reshape(1, HEAD_DIM))


def _nsa_cmp_body(T, n_cmp, n_sel, q_ref, kc_ref, vc_ref, c2s_ref, sl_ref, omul_ref, og_ref,
                  o_ref, sel_ref):
    i = pl.program_id(1)
    NC = kc_ref.shape[0]
    kc = kc_ref[...]
    vc = vc_ref[...]
    c2s = c2s_ref[...]
    row = lax.broadcasted_iota(jnp.int32, (T, NC), 0)
    col = lax.broadcasted_iota(jnp.int32, (T, NC), 1)
    dist = i * T + row - (col * NSA_CMP_STRIDE + (NSA_CMP_BLOCK - 1))
    ok = (dist >= 0) & (col < n_cmp)
    dist_f = dist.astype(F32)
    imp = jnp.zeros((T, LANES), F32)
    outs = []
    for p in range(NSA_HPG):
        qp = q_ref[:, p * LANES:(p + 1) * LANES]
        s = lax.dot_general(qp, kc, (((1,), (1,)), ((), ())), preferred_element_type=F32)
        s = jnp.where(ok, s - sl_ref[p] * dist_f, NEG_BIG)
        m = jnp.max(s, axis=-1, keepdims=True)
        e = jnp.where(ok, jnp.exp2(s - m), 0.0)
        l = jnp.sum(e, axis=-1, keepdims=True)
        prob = e / jnp.where(l > 0.0, l, 1.0)
        hi = prob.astype(BF16)
        lo = (prob - hi.astype(F32)).astype(BF16)
        o = jnp.dot(hi, vc, preferred_element_type=F32)
        imp = imp + (jnp.dot(hi, c2s, preferred_element_type=F32)
                     + jnp.dot(lo, c2s, preferred_element_type=F32))
        o = o * og_ref[p] * omul_ref[:, p * LANES:(p + 1) * LANES].astype(F32)
        outs.append(o.astype(o_ref.dtype))
    o_ref[...] = jnp.concatenate(outs, axis=-1)

    blk = lax.broadcasted_iota(jnp.int32, (T, LANES), 1)
    cur = (i * T + lax.broadcasted_iota(jnp.int32, (T, LANES), 0)) // NSA_SEL_BLOCK
    kn = min(NSA_SEL_N, n_sel)
    forced = ((blk == 0) | (blk == cur) | (blk == cur - 1)) & (blk <= cur)
    imp = jnp.where((blk > cur) | (blk >= n_sel) | forced, -jnp.inf, imp)
    taken = forced | _take_top(imp, blk, kn - 3)
    sel_ref[...] = jnp.where(taken, 0.0, NEG_BIG).astype(sel_ref.dtype)


def nsa_compressed(q, k_cmp, v_cmp, omul, og, T=256):
    S = q.shape[0]
    G, NC, _ = k_cmp.shape
    n_cmp = (S - NSA_CMP_BLOCK) // NSA_CMP_STRIDE + 1
    n_sel = S // NSA_SEL_BLOCK
    assert 3 <= n_sel <= LANES and NC >= n_cmp
    c_start = np.arange(NC)[:, None] * NSA_CMP_STRIDE
    s_start = np.arange(LANES)[None, :] * NSA_SEL_BLOCK
    c2s = ((c_start < s_start + NSA_SEL_BLOCK) & (c_start + NSA_CMP_BLOCK > s_start)
           & (np.arange(NC)[:, None] < n_cmp) & (np.arange(LANES)[None, :] < n_sel))
    slopes = np.broadcast_to(_alibi_slopes()[:, None, None] * LOG2E, (N_HEADS, 1, NC)).astype(np.float32)
    gw = NSA_HPG * LANES
    grp = pl.BlockSpec((T, gw), lambda g, i: (i, g))
    cmp_spec = pl.BlockSpec((None, NC, HEAD_DIM), lambda g, i: (g, 0, 0))
    return pl.pallas_call(
        functools.partial(_nsa_cmp_body, T, n_cmp, n_sel),
        out_shape=[jax.ShapeDtypeStruct((S, MIX_WIDTH), BF16),
                   jax.ShapeDtypeStruct((G, S, LANES), BF16)],
        grid=(G, S // T),
        in_specs=[grp, cmp_spec, cmp_spec,
                  pl.BlockSpec((NC, LANES), lambda g, i: (0, 0)),
                  pl.BlockSpec((NSA_HPG, 1, NC), lambda g, i: (g, 0, 0)),
                  grp,
                  pl.BlockSpec((NSA_HPG, T, 1), lambda g, i: (g, i, 0))],
        out_specs=[grp, pl.BlockSpec((None, T, LANES), lambda g, i: (g, i, 0))],
        compiler_params=_params("parallel", "parallel"),
        name="nsa_compressed",
    )(q, k_cmp, v_cmp, jnp.asarray(c2s.astype(np.float32), dtype=BF16), jnp.asarray(slopes),
      omul, og)


def _row(v):
    return v.reshape(1, -1).astype(F32)


def _rope_tables(S):
    inv_freq = ROPE_THETA ** (-jnp.arange(0, MLA_ROPE, 2, dtype=F32) / MLA_ROPE)
    ang = jnp.arange(S, dtype=F32)[:, None] * inv_freq[None, :]
    cos, sin = jnp.cos(ang), jnp.sin(ang)
    z = jnp.zeros_like(cos)
    return (jnp.concatenate([cos, z, cos, z], axis=-1),
            jnp.concatenate([-sin, z, sin, z], axis=-1))


def _rope_lanes(v):
    half = MLA_ROPE // 2
    z = jnp.zeros(v.shape[:-1] + (half,), v.dtype)
    return jnp.concatenate([v[..., :half], z, v[..., half:], z], axis=-1)


def mla_mixer(h, w_in, q_lat_norm, w_q_up, kv_lat_norm, w_kv_up, q_norm, k_norm):
    S = h.shape[0]
    c1 = MLA_Q_RANK
    c2 = c1 + MLA_KV_RANK
    c3 = c2 + MLA_ROPE
    scale = MLA_QK_DIM ** -0.5 * LOG2E
    cos, sin = _rope_tables(S)
    gq_n, gq_r = _row(q_norm[:MLA_NOPE]), _row(_rope_lanes(q_norm[MLA_NOPE:]))
    gk_n, gk_r = _row(k_norm[:MLA_NOPE]), _row(_rope_lanes(k_norm[MLA_NOPE:]))

    q_lat = proj(h, w_in[:, :c1].astype(BF16), _epi_rownorm, [(_row(q_lat_norm), "col")], name="mla_qlat")
    kv_lat = proj(h, w_in[:, c1:c2].astype(BF16), _epi_rownorm, [(_row(kv_lat_norm), "col")], name="mla_kvlat")
    k_rope = proj(h, _rope_lanes(w_in[:, c2:c3]).astype(BF16), _epi_mla_krope,
                  [(gk_r, "tile"), (cos, "row"), (sin, "row")], name="mla_krope")
    gate = proj(h, w_in[:, c3:].astype(BF16), _epi_silu, tn=1024, name="mla_gate")

    wq = w_q_up.reshape(MLA_Q_RANK, N_HEADS, MLA_QK_DIM)
    wq = jnp.concatenate([wq[..., :MLA_NOPE], _rope_lanes(wq[..., MLA_NOPE:])], axis=-1)
    q = proj(q_lat, wq.reshape(MLA_Q_RANK, N_HEADS * 2 * LANES).astype(BF16),
             functools.partial(_epi_mla_q, scale=scale),
             [(gq_n, "tile"), (gq_r, "tile"), (cos, "row"), (sin, "row")], tn=1024, name="mla_q")
    wkv = w_kv_up.reshape(MLA_KV_RANK, N_HEADS, MLA_NOPE + HEAD_DIM)
    k_nope = proj(kv_lat, wkv[..., :MLA_NOPE].reshape(MLA_KV_RANK, -1).astype(BF16),
                  functools.partial(_epi_headnorm, scale=1.0), [(gk_n, "tile")], tn=1024, name="mla_knope")
    v = proj(kv_lat, wkv[..., MLA_NOPE:].reshape(MLA_KV_RANK, -1).astype(BF16),
             functools.partial(_epi_none, scale=1.0), tn=1024, name="mla_v")
    return [flash(q, [(k_nope, True), (k_rope, False)], v, gate, dq=2 * LANES, kv_of=lambda hh: hh,
                  T=min(FLASH_T, S), name="mla_attention")]


def moba_mixer(h, w_in, q_norm, k_norm):
    S = h.shape[0]
    W = MIX_WIDTH
    scale = HEAD_DIM ** -0.5 * LOG2E
    q = proj(h, w_in[:, :W].astype(BF16), functools.partial(_epi_headnorm, scale=scale),
             [(_row(q_norm), "tile")], tn=1024, name="moba_q")
    k = proj(h, w_in[:, W:2 * W].astype(BF16), functools.partial(_epi_headnorm, scale=1.0),
             [(_row(k_norm), "tile")], tn=1024, name="moba_k")
    v = proj(h, w_in[:, 2 * W:3 * W].astype(BF16), functools.partial(_epi_none, scale=1.0), tn=1024, name="moba_v")
    gate = proj(h, w_in[:, 3 * W:].astype(BF16), _epi_silu, tn=1024, name="moba_gate")
    bias = moba_select(q, k)
    return [flash(q, [(k, True), (_block_onehot(S, MOBA_BLOCK), False)], v, gate, dq=LANES,
                  kv_of=lambda hh: hh, qx=bias, qx_of=lambda hh: hh, slopes=_alibi_slopes(),
                  T=min(FLASH_T, S), name="moba_attention")]


def stick_mixer(h, w_in):
    W = MIX_WIDTH
    scale = HEAD_DIM ** -0.5
    q = proj(h, w_in[:, :W].astype(BF16), functools.partial(_epi_none, scale=scale), tn=1024, name="sb_q")
    k = proj(h, w_in[:, W:2 * W].astype(BF16), functools.partial(_epi_none, scale=1.0), tn=1024, name="sb_k")
    v = proj(h, w_in[:, 2 * W:3 * W].astype(BF16), functools.partial(_epi_none, scale=1.0), tn=1024, name="sb_v")
    gate = proj(h, w_in[:, 3 * W:].astype(BF16), _epi_silu, tn=1024, name="sb_gate")
    return [stick_attention(q, k, v, gate)]


def nsa_mixer(h, w_in, q_norm, k_norm, w_cmp_k, w_cmp_v, cmp_pos):
    S = h.shape[0]
    G, P = NSA_GROUPS, NSA_HPG
    W, KV = MIX_WIDTH, NSA_KV_WIDTH
    half = KV // 2
    scale = HEAD_DIM ** -0.5 * LOG2E
    cuts = np.cumsum([W, KV, KV, KV, 3 * N_HEADS]).tolist()
    plain = functools.partial(_epi_none, scale=1.0)

    def knorm(b):
        return functools.partial(_epi_headnorm, scale=1.0), [(_row(k_norm[b]), "tile")]

    q = proj(h, w_in[:, :W].astype(BF16), functools.partial(_epi_headnorm, scale=scale),
             [(_row(q_norm), "tile")], tn=1024, name="nsa_q")
    kv_c = proj(h, w_in[:, cuts[0]:cuts[1]].astype(BF16), plain, name="nsa_kvc")
    ks = proj(h, w_in[:, cuts[1]:cuts[1] + half].astype(BF16), *knorm(1), name="nsa_ks")
    vs = proj(h, w_in[:, cuts[1] + half:cuts[2]].astype(BF16), plain, name="nsa_vs")
    kw = proj(h, w_in[:, cuts[2]:cuts[2] + half].astype(BF16), *knorm(2), name="nsa_kw")
    vw = proj(h, w_in[:, cuts[2] + half:cuts[3]].astype(BF16), plain, name="nsa_vw")
    wg = w_in[:, cuts[3]:cuts[4]].reshape(-1, N_HEADS, 3).transpose(0, 2, 1).reshape(-1, 3 * N_HEADS)
    wg = jnp.pad(wg, ((0, 0), (0, LANES - 3 * N_HEADS)))
    og = proj(h, wg.astype(BF16), _epi_sigmoid, out_dtype=F32, name="nsa_branch_gates")
    og = og[:, :3 * N_HEADS].T[:, :, None]
    gate = proj(h, w_in[:, cuts[4]:].astype(BF16), _epi_silu, tn=1024, name="nsa_gate")

    def chunks(t):
        return t.reshape(S // NSA_CMP_STRIDE, NSA_CMP_STRIDE, G, HEAD_DIM).transpose(2, 0, 1, 3).reshape(
            G, S // NSA_CMP_STRIDE, NSA_CMP_STRIDE * HEAD_DIM)

    k_cmp = nsa_compress(chunks(kv_c[:, :half]), w_cmp_k, cmp_pos, k_norm[0], True)
    v_cmp = nsa_compress(chunks(kv_c[:, half:]), w_cmp_v, cmp_pos, k_norm[0], False)
    o_cmp, bias = nsa_compressed(q, k_cmp, v_cmp, gate, og)
    slopes = _alibi_slopes()
    o_slc = flash(q, [(ks, True), (_block_onehot(S, NSA_SEL_BLOCK), False)], vs, gate, dq=LANES,
                  kv_of=lambda hh: hh // P, qx=bias, qx_of=lambda hh: hh // P, slopes=slopes,
                  og=og, og_row0=N_HEADS, T=min(FLASH_T, S), name="nsa_selected")
    o_win = flash(q, [(kw, True)], vw, gate, dq=LANES, kv_of=lambda hh: hh // P, slopes=slopes,
                  og=og, og_row0=2 * N_HEADS, window=NSA_WINDOW, T=min(WINDOW_T, S), name="nsa_window")
    return [o_cmp, o_slc, o_win]


def kernel(x, norm_a, w_in_a, q_lat_norm_a, w_q_up_a, kv_lat_norm_a, w_kv_up_a, q_norm_a, k_norm_a, w_out_a,
           norm_b, w_in_b, q_norm_b, k_norm_b, w_out_b,
           norm_c, w_in_c, w_out_c,
           norm_d, w_in_d, q_norm_d, k_norm_d, w_cmp_k_d, w_cmp_v_d, cmp_pos_d, w_out_d):
    B, S, D = x.shape
    norms = (norm_a, norm_b, norm_c, norm_d)
    w_outs = (w_out_a, w_out_b, w_out_c, w_out_d)
    depth = norm_a.shape[0] + norm_b.shape[0] + norm_c.shape[0] + norm_d.shape[0]
    outs = []
    for b in range(B):
        xb = x[b]
        h = rmsnorm_rows(xb, norms[0][0])
        for layer in range(depth):
            m, j = layer % 4, layer // 4
            if m == 0:
                a = mla_mixer(h, w_in_a[j], q_lat_norm_a[j], w_q_up_a[j], kv_lat_norm_a[j], w_kv_up_a[j],
                              q_norm_a[j], k_norm_a[j])
            elif m == 1:
                a = moba_mixer(h, w_in_b[j], q_norm_b[j], k_norm_b[j])
            elif m == 2:
                a = stick_mixer(h, w_in_c[j])
            else:
                a = nsa_mixer(h, w_in_d[j], q_norm_d[j], k_norm_d[j], w_cmp_k_d[j], w_cmp_v_d[j], cmp_pos_d[j])
            nxt = layer + 1
            g_next = norms[nxt % 4][nxt // 4] if nxt < depth else None
            xb, h = outproj(a, w_outs[m][j].astype(BF16), xb, g_next)
        outs.append(xb)
    return jnp.stack(outs, axis=0)
```

```python
import functools

import numpy as np
import jax
import jax.numpy as jnp
from jax import lax
from jax.experimental import pallas as pl
from jax.experimental.pallas import tpu as pltpu

N_HEADS = 16
HEAD_DIM = 128
MIX_WIDTH = N_HEADS * HEAD_DIM
EPS = 1e-6
NEG_BIG = -1e30
LANES = 128

MLA_Q_RANK = 512
MLA_KV_RANK = 256
MLA_NOPE = 128
MLA_ROPE = 64
MLA_QK_DIM = MLA_NOPE + MLA_ROPE
ROPE_THETA = 10000.0

MOBA_BLOCK = 256
MOBA_TOPK = 3

NSA_GROUPS = 4
NSA_HPG = N_HEADS // NSA_GROUPS
NSA_CMP_BLOCK = 32
NSA_CMP_STRIDE = 16
NSA_SEL_BLOCK = 64
NSA_SEL_N = 16
NSA_WINDOW = 512
NSA_KV_WIDTH = 2 * NSA_GROUPS * HEAD_DIM

SB_LOG_CUTOFF = -110.0

VMEM_LIMIT = 56 * 1024 * 1024

LOG2E = 1.4426950408889634

FLASH_T = 1024
WINDOW_T = 512
STICK_T = 256
PROJ_TM = 512
PROJ_TN = 1024

BF16 = jnp.bfloat16
F32 = jnp.float32


def _params(*sem):
    return pltpu.CompilerParams(dimension_semantics=sem, vmem_limit_bytes=VMEM_LIMIT)


def _alibi_slopes():
    return 2.0 ** (-8.0 * np.arange(1, N_HEADS + 1) / N_HEADS)


def _rmsnorm_body(x_ref, g_ref, o_ref):
    x = x_ref[...]
    ms = jnp.mean(x * x, axis=-1, keepdims=True)
    o_ref[...] = (x * lax.rsqrt(ms + EPS) * g_ref[...]).astype(o_ref.dtype)


def rmsnorm_rows(x, g, tm=512):
    S, D = x.shape
    return pl.pallas_call(
        _rmsnorm_body,
        out_shape=jax.ShapeDtypeStruct((S, D), BF16),
        grid=(S // tm,),
        in_specs=[pl.BlockSpec((tm, D), lambda i: (i, 0)),
                  pl.BlockSpec((1, D), lambda i: (0, 0))],
        out_specs=pl.BlockSpec((tm, D), lambda i: (i, 0)),
        compiler_params=_params("parallel"),
        name="rmsnorm_rows",
    )(x, g.reshape(1, D))


def _head_rms(y, gain, n_real):
    ms = jnp.sum(y * y, axis=-1, keepdims=True) * (1.0 / n_real)
    return y * lax.rsqrt(ms + EPS) * gain


def _cat(parts):
    return parts[0] if len(parts) == 1 else jnp.concatenate(parts, axis=-1)


def _epi_scale(scale):
    def epi(acc, ex):
        return acc * scale if scale != 1.0 else acc
    return epi


def _epi_silu(acc, ex):
    return acc / (1.0 + jnp.exp(-acc))


def _epi_sigmoid(acc, ex):
    return 1.0 / (1.0 + jnp.exp(-acc))


def _epi_headnorm(gain, scale=1.0, n_norm=None):
    def epi(acc, ex):
        n = acc.shape[-1] if n_norm is None else n_norm
        g = ex[gain][...] * scale
        parts = [_head_rms(acc[:, c:c + LANES], g, LANES) for c in range(0, n, LANES)]
        if n < acc.shape[-1]:
            parts.append(acc[:, n:])
        return _cat(parts)
    return epi


def _rope_slab(y, gain, cos, sin, scale):
    y = _head_rms(y, gain, MLA_ROPE)
    return (y * cos + pltpu.roll(y, 64, axis=1) * sin) * scale


def _epi_mla_q(scale):
    def epi(acc, ex):
        gn = ex["gq_n"][...] * scale
        gr, cos, sin = ex["gq_r"][...], ex["cos"][...], ex["sin"][...]
        parts = []
        for c in range(0, acc.shape[-1], 2 * LANES):
            parts.append(_head_rms(acc[:, c:c + LANES], gn, MLA_NOPE))
            parts.append(_rope_slab(acc[:, c + LANES:c + 2 * LANES], gr, cos, sin, scale))
        return _cat(parts)
    return epi


def _epi_mla_in(acc, ex):
    c1, c2 = MLA_Q_RANK, MLA_Q_RANK + MLA_KV_RANK
    q_lat = _head_rms(acc[:, :c1], ex["g_qlat"][...], c1)
    kv_lat = _head_rms(acc[:, c1:c2], ex["g_kvlat"][...], MLA_KV_RANK)
    k_rope = _rope_slab(acc[:, c2:c2 + LANES], ex["gk_r"][...], ex["cos"][...], ex["sin"][...], 1.0)
    return _cat([q_lat, kv_lat, k_rope, jnp.zeros_like(k_rope)])


def _proj_body(epis, names, cast, h_ref, w_ref, *refs):
    n = len(names)
    ex = dict(zip(names, refs[:n]))
    o_ref = refs[n]
    j, i = pl.program_id(0), pl.program_id(1)
    if cast:
        wb_ref = refs[n + 1]

        @pl.when(i == 0)
        def _():
            wb_ref[...] = w_ref[...].astype(BF16)

        w = wb_ref[...]
    else:
        w = w_ref[...]
    acc = jnp.dot(h_ref[...], w, preferred_element_type=F32)
    if len(epis) == 1:
        o_ref[...] = epis[0][2](acc, ex).astype(o_ref.dtype)
    else:
        for t0, t1, epi in epis:
            @pl.when((j >= t0) & (j < t1))
            def _(epi=epi):
                o_ref[...] = epi(acc, ex).astype(o_ref.dtype)


def proj(h, w, epis, extra=(), *, h_block=0, w_tile0=0, n_tiles=None, tn=PROJ_TN, out_dtype=BF16, name="proj"):
    S = h.shape[0]
    K = w.shape[0]
    tm = PROJ_TM
    n_tiles = (w.shape[1] // tn - w_tile0) if n_tiles is None else n_tiles
    cast = w.dtype != BF16
    in_specs = [pl.BlockSpec((tm, K), lambda j, i: (i, h_block)),
                pl.BlockSpec((K, tn), lambda j, i: (0, w_tile0 + j))]
    args = [h, w]
    names = []
    for nm, arr, kind in extra:
        names.append(nm)
        if kind == "vec":
            in_specs.append(pl.BlockSpec(arr.shape, lambda j, i: (0, 0)))
        else:
            in_specs.append(pl.BlockSpec((tm, LANES), lambda j, i: (i, 0)))
        args.append(arr)
    return pl.pallas_call(
        functools.partial(_proj_body, tuple(epis), tuple(names), cast),
        out_shape=jax.ShapeDtypeStruct((S, n_tiles * tn), out_dtype),
        grid=(n_tiles, S // tm),
        in_specs=in_specs,
        out_specs=pl.BlockSpec((tm, tn), lambda j, i: (i, j)),
        scratch_shapes=[pltpu.VMEM((K, tn), BF16)] if cast else [],
        compiler_params=_params("arbitrary", "arbitrary"),
        name=name,
    )(*args)


def _outproj_body(n_a, has_next, *refs):
    a_refs = refs[:n_a]
    w_ref, x_ref = refs[n_a], refs[n_a + 1]
    if has_next:
        g_ref, xo_ref, ho_ref = refs[n_a + 2:n_a + 5]
    else:
        xo_ref = refs[n_a + 2]
    if n_a == 1:
        a = a_refs[0][...]
    else:
        a = a_refs[0][...].astype(F32)
        for r in a_refs[1:]:
            a = a + r[...].astype(F32)
        a = a.astype(BF16)
    xn = x_ref[...] + jnp.dot(a, w_ref[...], preferred_element_type=F32)
    xo_ref[...] = xn
    if has_next:
        ms = jnp.mean(xn * xn, axis=-1, keepdims=True)
        ho_ref[...] = (xn * lax.rsqrt(ms + EPS) * g_ref[...]).astype(BF16)


def outproj(a_list, w, x, g_next=None, tm=256):
    S, D = x.shape
    n_a = len(a_list)
    has_next = g_next is not None
    row = pl.BlockSpec((tm, D), lambda i: (i, 0))
    in_specs = [pl.BlockSpec((tm, MIX_WIDTH), lambda i: (i, 0)) for _ in a_list]
    in_specs += [pl.BlockSpec((MIX_WIDTH, D), lambda i: (0, 0)), row]
    args = list(a_list) + [w, x]
    out_shape = [jax.ShapeDtypeStruct((S, D), F32)]
    out_specs = [row]
    if has_next:
        in_specs.append(pl.BlockSpec((1, D), lambda i: (0, 0)))
        args.append(g_next.reshape(1, D))
        out_shape.append(jax.ShapeDtypeStruct((S, D), BF16))
        out_specs.append(row)
    res = pl.pallas_call(
        functools.partial(_outproj_body, n_a, has_next),
        out_shape=out_shape,
        grid=(S // tm,),
        in_specs=in_specs,
        out_specs=out_specs,
        compiler_params=_params("parallel"),
        name="outproj",
    )(*args)
    return (res[0], res[1]) if has_next else (res[0], None)


def _flash_body(T, n_k, has_qx, has_alibi, has_og, *refs):
    pos = 0
    q_ref = refs[pos]; pos += 1
    qx_ref = None
    if has_qx:
        qx_ref = refs[pos]; pos += 1
    k_refs = refs[pos:pos + n_k]; pos += n_k
    v_ref = refs[pos]; pos += 1
    if has_alibi:
        al_ref, sl_ref = refs[pos], refs[pos + 1]; pos += 2
    omul_ref = refs[pos]; pos += 1
    if has_og:
        og_ref = refs[pos]; pos += 1
    o_ref = refs[pos]

    i = pl.program_id(1)
    q = q_ref[...]
    if has_qx:
        q = jnp.concatenate([q, qx_ref[...]], axis=-1)

    def tile(j, carry, masked):
        m, l, acc = carry
        start = pl.multiple_of(j * T, T)
        k = _cat([kr[pl.ds(start, T), :] for kr in k_refs])
        s = lax.dot_general(q, k, (((1,), (1,)), ((), ())), preferred_element_type=F32)
        if has_alibi:
            s = s + (al_ref[...] + sl_ref[...] * ((j - i) * T).astype(F32))
        if masked:
            rows = lax.broadcasted_iota(jnp.int32, (T, T), 0)
            cols = lax.broadcasted_iota(jnp.int32, (T, T), 1)
            s = jnp.where(rows >= cols, s, NEG_BIG)
        m_new = jnp.maximum(m, jnp.max(s, axis=-1, keepdims=True))
        alpha = jnp.exp2(m - m_new)
        p = jnp.exp2(s - m_new)
        l = alpha * l + jnp.sum(p, axis=-1, keepdims=True)
        acc = alpha * acc + jnp.dot(p.astype(BF16), v_ref[pl.ds(start, T), :],
                                    preferred_element_type=F32)
        return m_new, l, acc

    init = (jnp.full((T, 1), NEG_BIG, F32), jnp.zeros((T, 1), F32), jnp.zeros((T, LANES), F32))
    carry = tile(i, init, True)
    _, l, acc = lax.fori_loop(0, i, lambda j, c: tile(j, c, False), carry)
    o = acc / l * omul_ref[...].astype(F32)
    if has_og:
        o = o * og_ref[...]
    o_ref[...] = o.astype(o_ref.dtype)


def _slab_spec(rows, src, head_of, width=LANES):
    off = src[1] * LANES // width
    if rows is None:
        return pl.BlockSpec((src[0].shape[0], width), lambda h, i: (0, off + head_of(h)))
    return pl.BlockSpec((rows, width), lambda h, i: (i, off + head_of(h)))


def _same(h):
    return h


def _first(h):
    return 0


def flash(q, k_parts, v, omul, *, dq, kv_of, T, qx=None, qx_of=None, slopes=None,
          og=None, og_row0=0, name="flash"):
    S = q[0].shape[0]
    H = N_HEADS
    in_specs = [_slab_spec(T, q, _same, dq)]
    args = [q[0]]
    if qx is not None:
        in_specs.append(pl.BlockSpec((None, T, LANES), lambda h, i: (qx_of(h), i, 0)))
        args.append(qx)
    for src, per_head in k_parts:
        in_specs.append(_slab_spec(None, src, kv_of if per_head else _first))
        args.append(src[0])
    in_specs.append(_slab_spec(None, v, kv_of))
    args.append(v[0])
    if slopes is not None:
        sl = np.asarray(slopes, np.float64)[:, None, None] * LOG2E
        al = (sl * np.arange(T)[None, None, :]).astype(np.float32)
        sl = np.broadcast_to(sl, (H, 1, T)).astype(np.float32)
        in_specs += [pl.BlockSpec((None, 1, T), lambda h, i: (h, 0, 0))] * 2
        args += [jnp.asarray(al), jnp.asarray(sl)]
    in_specs.append(_slab_spec(T, omul, _same))
    args.append(omul[0])
    if og is not None:
        in_specs.append(pl.BlockSpec((None, T, 1), lambda h, i: (og_row0 + h, i, 0)))
        args.append(og)
    body = functools.partial(_flash_body, T, len(k_parts), qx is not None, slopes is not None,
                             og is not None)
    return pl.pallas_call(
        body,
        out_shape=jax.ShapeDtypeStruct((S, H * LANES), BF16),
        grid=(H, S // T),
        in_specs=in_specs,
        out_specs=pl.BlockSpec((T, LANES), lambda h, i: (i, h)),
        compiler_params=_params("parallel", "arbitrary"),
        name=name,
    )(*args)


def _window_body(T, W, q_ref, k_ref, v_ref, al_ref, sl_ref, omul_ref, og_ref, o_ref):
    i = pl.program_id(1)
    span = W + T
    q0 = i * T
    start = pl.multiple_of(jnp.maximum(q0 - W, 0), T)
    s = lax.dot_general(q_ref[...], k_ref[pl.ds(start, span), :], (((1,), (1,)), ((), ())),
                        preferred_element_type=F32)
    s = s + (al_ref[...] + sl_ref[...] * (start - q0).astype(F32))
    rows = lax.broadcasted_iota(jnp.int32, (T, span), 0)
    cols = lax.broadcasted_iota(jnp.int32, (T, span), 1)
    dist = rows - cols + (q0 - start)
    s = jnp.where((dist >= 0) & (dist < W), s, NEG_BIG)
    p = jnp.exp2(s - jnp.max(s, axis=-1, keepdims=True))
    l = jnp.sum(p, axis=-1, keepdims=True)
    o = jnp.dot(p.astype(BF16), v_ref[pl.ds(start, span), :], preferred_element_type=F32)
    o_ref[...] = (o / l * omul_ref[...].astype(F32) * og_ref[...]).astype(o_ref.dtype)


def window_attention(q, k, v, omul, og, og_row0, *, kv_of, T, W=NSA_WINDOW):
    S = q[0].shape[0]
    span = W + T
    assert W % T == 0 and S >= span
    sl = _alibi_slopes()[:, None, None] * LOG2E
    al = (sl * np.arange(span)[None, None, :]).astype(np.float32)
    sl = np.broadcast_to(sl, (N_HEADS, 1, span)).astype(np.float32)
    vec = pl.BlockSpec((None, 1, span), lambda h, i: (h, 0, 0))
    return pl.pallas_call(
        functools.partial(_window_body, T, W),
        out_shape=jax.ShapeDtypeStruct((S, N_HEADS * LANES), BF16),
        grid=(N_HEADS, S // T),
        in_specs=[_slab_spec(T, q, _same), _slab_spec(None, k, kv_of), _slab_spec(None, v, kv_of),
                  vec, vec, _slab_spec(T, omul, _same),
                  pl.BlockSpec((None, T, 1), lambda h, i: (og_row0 + h, i, 0))],
        out_specs=pl.BlockSpec((T, LANES), lambda h, i: (i, h)),
        compiler_params=_params("parallel", "arbitrary"),
        name="nsa_window",
    )(q[0], k[0], v[0], jnp.asarray(al), jnp.asarray(sl), omul[0], og)


def _block_onehot(S, block):
    e = (np.arange(S)[:, None] // block) == np.arange(LANES)[None, :]
    return jnp.asarray(e.astype(np.float32), dtype=BF16)


def _take_top(score, col, n):
    colf = col.astype(F32)
    left = score
    for _ in range(n):
        m = jnp.max(left, axis=-1, keepdims=True)
        idx = jnp.min(jnp.where(left == m, colf, float(LANES)), axis=-1, keepdims=True)
        left = jnp.where(colf == idx, -jnp.inf, left)
    return (left == -jnp.inf) & (score > -jnp.inf)


def _moba_select_body(T, NB, q_ref, k_ref, o_ref, km_ref):
    i = pl.program_id(1)

    @pl.when(i == 0)
    def _():
        km_ref[...] = jnp.zeros_like(km_ref)
        for b in range(NB):
            kb = k_ref[b * MOBA_BLOCK:(b + 1) * MOBA_BLOCK, :].astype(F32)
            km_ref[b:b + 1, :] = jnp.sum(kb, axis=0, keepdims=True) * (1.0 / MOBA_BLOCK)

    q = q_ref[...]
    km = km_ref[...]
    km_hi = km.astype(BF16)
    km_lo = (km - km_hi.astype(F32)).astype(BF16)
    dn = (((1,), (1,)), ((), ()))
    score = (lax.dot_general(q, km_hi, dn, preferred_element_type=F32)
             + lax.dot_general(q, km_lo, dn, preferred_element_type=F32))
    col = lax.broadcasted_iota(jnp.int32, (T, LANES), 1)
    row = lax.broadcasted_iota(jnp.int32, (T, LANES), 0)
    q_blk = (i * T + row) // MOBA_BLOCK
    score = jnp.where(col < q_blk, score, -jnp.inf)
    taken = _take_top(score, col, min(MOBA_TOPK, NB)) | (col == q_blk)
    o_ref[...] = jnp.where(taken, 0.0, NEG_BIG).astype(o_ref.dtype)


def moba_select(q, k, T=512):
    S = q[0].shape[0]
    NB = S // MOBA_BLOCK
    return pl.pallas_call(
        functools.partial(_moba_select_body, T, NB),
        out_shape=jax.ShapeDtypeStruct((N_HEADS, S, LANES), BF16),
        grid=(N_HEADS, S // T),
        in_specs=[_slab_spec(T, q, _same), _slab_spec(None, k, _same)],
        out_specs=pl.BlockSpec((None, T, LANES), lambda h, i: (h, i, 0)),
        scratch_shapes=[pltpu.VMEM((LANES, LANES), F32)],
        compiler_params=_params("parallel", "arbitrary"),
        name="moba_select",
    )(q[0], k[0])


def _softplus(z):
    return jnp.maximum(z, 0.0) + jnp.log(1.0 + jnp.exp(-jnp.abs(z)))


def _stick_body(T, NH, q_ref, k_ref, v_ref, omul_ref, o_ref):
    i = pl.program_id(1)
    rows = lax.broadcasted_iota(jnp.int32, (T, T), 0)
    cols = lax.broadcasted_iota(jnp.int32, (T, T), 1)
    later = jnp.where(rows > cols, 1.0, 0.0).astype(BF16)
    before = cols < rows
    lanes = [slice(n * LANES, (n + 1) * LANES) for n in range(NH)]
    qs = [q_ref[:, ln] for ln in lanes]

    def tile(j, n, run, acc, diag):
        start = pl.multiple_of(j * T, T)
        k = k_ref[pl.ds(start, T), lanes[n]]
        z = lax.dot_general(qs[n], k, (((1,), (1,)), ((), ())), preferred_element_type=F32)
        log_keep = -_softplus(z)
        if diag:
            log_keep = jnp.where(before, log_keep, 0.0)
        hi = log_keep.astype(BF16)
        lo = (log_keep - hi.astype(F32)).astype(BF16)
        between = (jnp.dot(hi, later, preferred_element_type=F32)
                   + jnp.dot(lo, later, preferred_element_type=F32))
        a = jnp.exp(z + log_keep + between + run)
        if diag:
            a = jnp.where(before, a, 0.0)
        acc = acc + jnp.dot(a.astype(BF16), v_ref[pl.ds(start, T), lanes[n]], preferred_element_type=F32)
        run = run + jnp.sum(log_keep, axis=-1, keepdims=True)
        return run, acc

    state = tuple(tile(i, n, jnp.zeros((T, 1), F32), jnp.zeros((T, LANES), F32), True) for n in range(NH))

    def cond(st):
        j, state = st
        top = state[0][0]
        for run, _ in state[1:]:
            top = jnp.maximum(top, run)
        return jnp.logical_and(j >= 0, jnp.max(top) > SB_LOG_CUTOFF)

    def step(st):
        j, state = st
        return j - 1, tuple(tile(j, n, run, acc, False) for n, (run, acc) in enumerate(state))

    _, state = lax.while_loop(cond, step, (i - 1, state))
    o = _cat([acc for _, acc in state])
    o_ref[...] = (o * omul_ref[...].astype(F32)).astype(o_ref.dtype)


def stick_attention(q, k, v, omul, T, NH=2):
    S = q[0].shape[0]
    width = NH * LANES

    def pair(h):
        return h

    grid = (N_HEADS // NH, S // T)
    return pl.pallas_call(
        functools.partial(_stick_body, T, NH),
        out_shape=jax.ShapeDtypeStruct((S, MIX_WIDTH), BF16),
        grid=grid,
        in_specs=[_slab_spec(T, q, pair, width), _slab_spec(None, k, pair, width),
                  _slab_spec(None, v, pair, width), _slab_spec(T, omul, pair, width)],
        out_specs=pl.BlockSpec((T, width), lambda h, i: (i, h)),
        compiler_params=_params("parallel", "arbitrary"),
        name="stick_attention",
    )(q[0], k[0], v[0], omul[0])


def _nsa_compress_body(norm, x_ref, w_ref, pos_ref, g_ref, o_ref):
    half = NSA_CMP_STRIDE * HEAD_DIM
    x = x_ref[...]
    first = jnp.dot(x, w_ref[:half, :], preferred_element_type=F32)
    second = jnp.dot(x, w_ref[half:, :], preferred_element_type=F32)
    bias = jnp.dot(pos_ref[...], w_ref[...], preferred_element_type=F32)
    n = x.shape[0]
    y = first + pltpu.roll(second, n - 1, axis=0) + bias[0:1, :]
    if norm:
        y = _head_rms(y, g_ref[...], HEAD_DIM)
    o_ref[...] = y.astype(o_ref.dtype)


def nsa_compress(x, w, pos, gain, norm):
    G, n, width = x.shape
    pos_flat = jnp.broadcast_to(pos.reshape(1, -1), (16, pos.size)).astype(BF16)
    return pl.pallas_call(
        functools.partial(_nsa_compress_body, norm),
        out_shape=jax.ShapeDtypeStruct((G, n, HEAD_DIM), BF16),
        grid=(G,),
        in_specs=[pl.BlockSpec((None, n, width), lambda g: (g, 0, 0)),
                  pl.BlockSpec(w.shape, lambda g: (0, 0)),
                  pl.BlockSpec((16, pos.size), lambda g: (0, 0)),
                  pl.BlockSpec((1, HEAD_DIM), lambda g: (0, 0))],
        out_specs=pl.BlockSpec((None, n, HEAD_DIM), lambda g: (g, 0, 0)),
        compiler_params=_params("parallel"),
        name="nsa_compress",
    )(x, w.astype(BF16), pos_flat, gain.reshape(1, HEAD_DIM))


def _nsa_cmp_body(T, n_cmp, n_sel, q_ref, kc_ref, vc_ref, c2s_ref, sl_ref, omul_ref, og_ref,
                  o_ref, sel_ref):
    i = pl.program_id(1)
    NC = kc_ref.shape[0]
    kc = kc_ref[...]
    vc = vc_ref[...]
    c2s = c2s_ref[...]
    row = lax.broadcasted_iota(jnp.int32, (T, NC), 0)
    col = lax.broadcasted_iota(jnp.int32, (T, NC), 1)
    dist = i * T + row - (col * NSA_CMP_STRIDE + (NSA_CMP_BLOCK - 1))
    ok = (dist >= 0) & (col < n_cmp)
    dist_f = dist.astype(F32)
    imp = jnp.zeros((T, LANES), F32)
    outs = []
    for p in range(NSA_HPG):
        qp = q_ref[:, p * LANES:(p + 1) * LANES]
        s = lax.dot_general(qp, kc, (((1,), (1,)), ((), ())), preferred_element_type=F32)
        s = jnp.where(ok, s - sl_ref[p] * dist_f, NEG_BIG)
        m = jnp.max(s, axis=-1, keepdims=True)
        e = jnp.where(ok, jnp.exp2(s - m), 0.0)
        l = jnp.sum(e, axis=-1, keepdims=True)
        prob = e / jnp.where(l > 0.0, l, 1.0)
        hi = prob.astype(BF16)
        lo = (prob - hi.astype(F32)).astype(BF16)
        o = jnp.dot(hi, vc, preferred_element_type=F32)
        imp = imp + (jnp.dot(hi, c2s, preferred_element_type=F32)
                     + jnp.dot(lo, c2s, preferred_element_type=F32))
        o = o * og_ref[p] * omul_ref[:, p * LANES:(p + 1) * LANES].astype(F32)
        outs.append(o.astype(o_ref.dtype))
    o_ref[...] = jnp.concatenate(outs, axis=-1)

    blk = lax.broadcasted_iota(jnp.int32, (T, LANES), 1)
    cur = (i * T + lax.broadcasted_iota(jnp.int32, (T, LANES), 0)) // NSA_SEL_BLOCK
    kn = min(NSA_SEL_N, n_sel)
    forced = ((blk == 0) | (blk == cur) | (blk == cur - 1)) & (blk <= cur)
    imp = jnp.where((blk > cur) | (blk >= n_sel) | forced, -jnp.inf, imp)
    taken = forced | _take_top(imp, blk, kn - 3)
    sel_ref[...] = jnp.where(taken, 0.0, NEG_BIG).astype(sel_ref.dtype)


def nsa_compressed(q, k_cmp, v_cmp, omul, og, T=256):
    S = q[0].shape[0]
    G, NC, _ = k_cmp.shape
    n_cmp = (S - NSA_CMP_BLOCK) // NSA_CMP_STRIDE + 1
    n_sel = S // NSA_SEL_BLOCK
    assert 3 <= n_sel <= LANES and NC >= n_cmp
    c_start = np.arange(NC)[:, None] * NSA_CMP_STRIDE
    s_start = np.arange(LANES)[None, :] * NSA_SEL_BLOCK
    c2s = ((c_start < s_start + NSA_SEL_BLOCK) & (c_start + NSA_CMP_BLOCK > s_start)
           & (np.arange(NC)[:, None] < n_cmp) & (np.arange(LANES)[None, :] < n_sel))
    slopes = np.broadcast_to(_alibi_slopes()[:, None, None] * LOG2E, (N_HEADS, 1, NC)).astype(np.float32)
    gw = NSA_HPG * LANES
    cmp_spec = pl.BlockSpec((None, NC, HEAD_DIM), lambda g, i: (g, 0, 0))
    return pl.pallas_call(
        functools.partial(_nsa_cmp_body, T, n_cmp, n_sel),
        out_shape=[jax.ShapeDtypeStruct((S, MIX_WIDTH), BF16),
                   jax.ShapeDtypeStruct((G, S, LANES), BF16)],
        grid=(G, S // T),
        in_specs=[_slab_spec(T, q, _same, gw), cmp_spec, cmp_spec,
                  pl.BlockSpec((NC, LANES), lambda g, i: (0, 0)),
                  pl.BlockSpec((NSA_HPG, 1, NC), lambda g, i: (g, 0, 0)),
                  _slab_spec(T, omul, _same, gw),
                  pl.BlockSpec((NSA_HPG, T, 1), lambda g, i: (g, i, 0))],
        out_specs=[pl.BlockSpec((T, gw), lambda g, i: (i, g)),
                   pl.BlockSpec((None, T, LANES), lambda g, i: (g, i, 0))],
        compiler_params=_params("parallel", "parallel"),
        name="nsa_compressed",
    )(q[0], k_cmp, v_cmp, jnp.asarray(c2s.astype(np.float32), dtype=BF16), jnp.asarray(slopes),
      omul[0], og)


def _row(v):
    return v.reshape(1, -1).astype(F32)


def _rope_tables(S):
    inv_freq = ROPE_THETA ** (-jnp.arange(0, MLA_ROPE, 2, dtype=F32) / MLA_ROPE)
    ang = jnp.arange(S, dtype=F32)[:, None] * inv_freq[None, :]
    cos, sin = jnp.cos(ang), jnp.sin(ang)
    z = jnp.zeros_like(cos)
    return (jnp.concatenate([cos, z, cos, z], axis=-1),
            jnp.concatenate([-sin, z, sin, z], axis=-1))


def _rope_lanes(v):
    half = MLA_ROPE // 2
    z = jnp.zeros(v.shape[:-1] + (half,), v.dtype)
    return jnp.concatenate([v[..., :half], z, v[..., half:], z], axis=-1)


def mla_mixer(h, w_in, q_lat_norm, w_q_up, kv_lat_norm, w_kv_up, q_norm, k_norm):
    S = h.shape[0]
    c1 = MLA_Q_RANK
    c2 = c1 + MLA_KV_RANK
    c3 = c2 + MLA_ROPE
    scale = MLA_QK_DIM ** -0.5 * LOG2E
    cos, sin = _rope_tables(S)
    tables = [("cos", cos, "row"), ("sin", sin, "row")]

    w_lat = jnp.concatenate([w_in[:, :c2], _rope_lanes(w_in[:, c2:c3]),
                             jnp.zeros((w_in.shape[0], PROJ_TN - c2 - LANES), w_in.dtype)], axis=-1)
    lat = proj(h, w_lat, [(0, 1, _epi_mla_in)],
               [("g_qlat", _row(q_lat_norm), "vec"), ("g_kvlat", _row(kv_lat_norm), "vec"),
                ("gk_r", _row(_rope_lanes(k_norm[MLA_NOPE:])), "vec")] + tables, name="mla_latents")
    gate = proj(h, w_in[:, c3:], [(0, 2, _epi_silu)], name="mla_gate")

    wq = w_q_up.reshape(MLA_Q_RANK, N_HEADS, MLA_QK_DIM)
    wq = jnp.concatenate([wq[..., :MLA_NOPE], _rope_lanes(wq[..., MLA_NOPE:])], axis=-1)
    q = proj(lat, wq.reshape(MLA_Q_RANK, N_HEADS * 2 * LANES).astype(BF16), [(0, 4, _epi_mla_q(scale))],
             [("gq_n", _row(q_norm[:MLA_NOPE]), "vec"), ("gq_r", _row(_rope_lanes(q_norm[MLA_NOPE:])), "vec")]
             + tables, name="mla_q")
    wkv = w_kv_up.reshape(MLA_KV_RANK, N_HEADS, MLA_NOPE + HEAD_DIM)
    wkv = jnp.concatenate([wkv[..., :MLA_NOPE].reshape(MLA_KV_RANK, -1),
                           wkv[..., MLA_NOPE:].reshape(MLA_KV_RANK, -1)], axis=-1).astype(BF16)
    kv = proj(lat, wkv, [(0, 2, _epi_headnorm("gk_n")), (2, 4, _epi_scale(1.0))],
              [("gk_n", _row(k_norm[:MLA_NOPE]), "vec")], h_block=c1 // MLA_KV_RANK, name="mla_kv")
    k_rope = (lat, c2 // LANES)
    return [flash((q, 0), [((kv, 0), True), (k_rope, False)], (kv, N_HEADS), (gate, 0), dq=2 * LANES,
                  kv_of=_same, T=min(FLASH_T, S), name="mla_attention")]


def moba_mixer(h, w_in, q_norm, k_norm):
    S = h.shape[0]
    scale = HEAD_DIM ** -0.5 * LOG2E
    n = MIX_WIDTH // PROJ_TN
    qkvg = proj(h, w_in, [(0, n, _epi_headnorm("gq", scale)), (n, 2 * n, _epi_headnorm("gk")),
                          (2 * n, 3 * n, _epi_scale(1.0)), (3 * n, 4 * n, _epi_silu)],
                [("gq", _row(q_norm), "vec"), ("gk", _row(k_norm), "vec")], name="moba_in")
    H = N_HEADS
    q, k, v, gate = (qkvg, 0), (qkvg, H), (qkvg, 2 * H), (qkvg, 3 * H)
    bias = moba_select(q, k)
    return [flash(q, [(k, True), ((_block_onehot(S, MOBA_BLOCK), 0), False)], v, gate, dq=LANES,
                  kv_of=_same, qx=bias, qx_of=_same, slopes=_alibi_slopes(),
                  T=min(FLASH_T, S), name="moba_attention")]


def stick_mixer(h, w_in):
    S = h.shape[0]
    scale = HEAD_DIM ** -0.5
    n = MIX_WIDTH // PROJ_TN
    qkvg = proj(h, w_in, [(0, n, _epi_scale(scale)), (n, 3 * n, _epi_scale(1.0)), (3 * n, 4 * n, _epi_silu)],
                name="stick_in")
    H = N_HEADS
    return [stick_attention((qkvg, 0), (qkvg, H), (qkvg, 2 * H), (qkvg, 3 * H), T=min(STICK_T, S))]


def nsa_mixer(h, w_in, q_norm, k_norm, w_cmp_k, w_cmp_v, cmp_pos):
    S = h.shape[0]
    G, P, H = NSA_GROUPS, NSA_HPG, N_HEADS
    W, KV = MIX_WIDTH, NSA_KV_WIDTH
    half = KV // 2
    scale = HEAD_DIM ** -0.5 * LOG2E
    cuts = np.cumsum([W, KV, KV, KV, 3 * H]).tolist()
    assert KV == PROJ_TN and W % PROJ_TN == 0
    n = W // PROJ_TN
    main = proj(h, w_in, [(0, n, _epi_headnorm("gq", scale)), (n, n + 1, _epi_scale(1.0)),
                          (n + 1, n + 2, _epi_headnorm("gk1", n_norm=half)),
                          (n + 2, n + 3, _epi_headnorm("gk2", n_norm=half))],
                [("gq", _row(q_norm), "vec"), ("gk1", _row(k_norm[1]), "vec"), ("gk2", _row(k_norm[2]), "vec")],
                n_tiles=n + 3, name="nsa_in")
    q = (main, 0)
    kc0 = H
    ks, vs = (main, kc0 + 2 * G), (main, kc0 + 3 * G)
    kw, vw = (main, kc0 + 4 * G), (main, kc0 + 5 * G)
    wg = w_in[:, cuts[3]:cuts[4]].reshape(-1, H, 3).transpose(0, 2, 1).reshape(-1, 3 * H)
    wg = jnp.pad(wg, ((0, 0), (0, LANES - 3 * H)))
    og = proj(h, wg.astype(BF16), [(0, 1, _epi_sigmoid)], tn=LANES, out_dtype=F32, name="nsa_branch_gates")
    og = og[:, :3 * H].T[:, :, None]
    gate = (proj(h, w_in[:, cuts[4]:], [(0, n, _epi_silu)], name="nsa_gate"), 0)

    def chunks(t):
        return t.reshape(S // NSA_CMP_STRIDE, NSA_CMP_STRIDE, G, HEAD_DIM).transpose(2, 0, 1, 3).reshape(
            G, S // NSA_CMP_STRIDE, NSA_CMP_STRIDE * HEAD_DIM)

    c0 = kc0 * LANES
    k_cmp = nsa_compress(chunks(main[:, c0:c0 + half]), w_cmp_k, cmp_pos, k_norm[0], True)
    v_cmp = nsa_compress(chunks(main[:, c0 + half:c0 + KV]), w_cmp_v, cmp_pos, k_norm[0], False)
    o_cmp, bias = nsa_compressed(q, k_cmp, v_cmp, gate, og)

    def group_of(hh):
        return hh // P

    o_slc = flash(q, [(ks, True), ((_block_onehot(S, NSA_SEL_BLOCK), 0), False)], vs, gate, dq=LANES,
                  kv_of=group_of, qx=bias, qx_of=group_of, slopes=_alibi_slopes(),
                  og=og, og_row0=H, T=min(FLASH_T, S), name="nsa_selected")
    o_win = window_attention(q, kw, vw, gate, og, 2 * H, kv_of=group_of, T=min(WINDOW_T, S))
    return [o_cmp, o_slc, o_win]


def kernel(x, norm_a, w_in_a, q_lat_norm_a, w_q_up_a, kv_lat_norm_a, w_kv_up_a, q_norm_a, k_norm_a, w_out_a,
           norm_b, w_in_b, q_norm_b, k_norm_b, w_out_b,
           norm_c, w_in_c, w_out_c,
           norm_d, w_in_d, q_norm_d, k_norm_d, w_cmp_k_d, w_cmp_v_d, cmp_pos_d, w_out_d):
    B, S, D = x.shape
    norms = (norm_a, norm_b, norm_c, norm_d)
    w_outs = (w_out_a, w_out_b, w_out_c, w_out_d)
    depth = norm_a.shape[0] + norm_b.shape[0] + norm_c.shape[0] + norm_d.shape[0]
    rows = x.reshape(B * S, D)
    outs = []
    for b in range(B):
        xb = rows if B == 1 else rows[b * S:(b + 1) * S]
        h = rmsnorm_rows(xb, norms[0][0])
        for layer in range(depth):
            m, j = layer % 4, layer // 4
            if m == 0:
                a = mla_mixer(h, w_in_a[j], q_lat_norm_a[j], w_q_up_a[j], kv_lat_norm_a[j], w_kv_up_a[j],
                              q_norm_a[j], k_norm_a[j])
            elif m == 1:
                a = moba_mixer(h, w_in_b[j], q_norm_b[j], k_norm_b[j])
            elif m == 2:
                a = stick_mixer(h, w_in_c[j])
            else:
                a = nsa_mixer(h, w_in_d[j], q_norm_d[j], k_norm_d[j], w_cmp_k_d[j], w_cmp_v_d[j], cmp_pos_d[j])
            nxt = layer + 1
            g_next = norms[nxt % 4][nxt // 4] if nxt < depth else None
            xb, h = outproj(a, w_outs[m][j].astype(BF16), xb, g_next)
        outs.append(xb)
    out = outs[0] if B == 1 else jnp.concatenate(outs, axis=0)
    return out.reshape(B, S, D)
```

```python
import functools

import numpy as np
import jax
import jax.numpy as jnp
from jax import lax
from jax.experimental import pallas as pl
from jax.experimental.pallas import tpu as pltpu

N_HEADS = 16
HEAD_DIM = 128
MIX_WIDTH = N_HEADS * HEAD_DIM
EPS = 1e-6
NEG_BIG = -1e30
LANES = 128

MLA_Q_RANK = 512
MLA_KV_RANK = 256
MLA_NOPE = 128
MLA_ROPE = 64
MLA_QK_DIM = MLA_NOPE + MLA_ROPE
ROPE_THETA = 10000.0

MOBA_BLOCK = 256
MOBA_TOPK = 3

NSA_GROUPS = 4
NSA_HPG = N_HEADS // NSA_GROUPS
NSA_CMP_BLOCK = 32
NSA_CMP_STRIDE = 16
NSA_SEL_BLOCK = 64
NSA_SEL_N = 16
NSA_WINDOW = 512
NSA_KV_WIDTH = 2 * NSA_GROUPS * HEAD_DIM

SB_LOG_CUTOFF = -110.0

VMEM_LIMIT = 56 * 1024 * 1024

LOG2E = 1.4426950408889634

FLASH_T = 1024
WINDOW_T = 512
STICK_T = 256
NSA_CMP_T = 512
FLASH_HEADS = 2
STICK_HEADS = 4
PROJ_TM = 512
PROJ_TN = 1024

BF16 = jnp.bfloat16
F32 = jnp.float32


def _params(*sem):
    return pltpu.CompilerParams(dimension_semantics=sem, vmem_limit_bytes=VMEM_LIMIT)


def _alibi_slopes():
    return 2.0 ** (-8.0 * np.arange(1, N_HEADS + 1) / N_HEADS)


def _rmsnorm_body(x_ref, g_ref, o_ref):
    x = x_ref[...]
    ms = jnp.mean(x * x, axis=-1, keepdims=True)
    o_ref[...] = (x * lax.rsqrt(ms + EPS) * g_ref[...]).astype(o_ref.dtype)


def rmsnorm_rows(x, g, tm=512):
    S, D = x.shape
    return pl.pallas_call(
        _rmsnorm_body,
        out_shape=jax.ShapeDtypeStruct((S, D), BF16),
        grid=(S // tm,),
        in_specs=[pl.BlockSpec((tm, D), lambda i: (i, 0)),
                  pl.BlockSpec((1, D), lambda i: (0, 0))],
        out_specs=pl.BlockSpec((tm, D), lambda i: (i, 0)),
        compiler_params=_params("parallel"),
        name="rmsnorm_rows",
    )(x, g.reshape(1, D))


def _head_rms(y, gain, n_real):
    ms = jnp.sum(y * y, axis=-1, keepdims=True) * (1.0 / n_real)
    return y * lax.rsqrt(ms + EPS) * gain


def _cat(parts):
    return parts[0] if len(parts) == 1 else jnp.concatenate(parts, axis=-1)


def _epi_scale(scale):
    def epi(acc, ex):
        return acc * scale if scale != 1.0 else acc
    return epi


def _epi_silu(acc, ex):
    return acc / (1.0 + jnp.exp(-acc))


def _epi_sigmoid(acc, ex):
    return 1.0 / (1.0 + jnp.exp(-acc))


def _epi_headnorm(gain, scale=1.0, n_norm=None):
    def epi(acc, ex):
        n = acc.shape[-1] if n_norm is None else n_norm
        g = ex[gain][...] * scale
        parts = [_head_rms(acc[:, c:c + LANES], g, LANES) for c in range(0, n, LANES)]
        if n < acc.shape[-1]:
            parts.append(acc[:, n:])
        return _cat(parts)
    return epi


def _rope_slab(y, gain, cos, sin, scale):
    y = _head_rms(y, gain, MLA_ROPE)
    return (y * cos + pltpu.roll(y, 64, axis=1) * sin) * scale


def _epi_mla_q(scale):
    def epi(acc, ex):
        gn = ex["gq_n"][...] * scale
        gr, cos, sin = ex["gq_r"][...], ex["cos"][...], ex["sin"][...]
        parts = []
        for c in range(0, acc.shape[-1], 2 * LANES):
            parts.append(_head_rms(acc[:, c:c + LANES], gn, MLA_NOPE))
            parts.append(_rope_slab(acc[:, c + LANES:c + 2 * LANES], gr, cos, sin, scale))
        return _cat(parts)
    return epi


def _epi_mla_in(acc, ex):
    c1, c2 = MLA_Q_RANK, MLA_Q_RANK + MLA_KV_RANK
    q_lat = _head_rms(acc[:, :c1], ex["g_qlat"][...], c1)
    kv_lat = _head_rms(acc[:, c1:c2], ex["g_kvlat"][...], MLA_KV_RANK)
    k_rope = _rope_slab(acc[:, c2:c2 + LANES], ex["gk_r"][...], ex["cos"][...], ex["sin"][...], 1.0)
    return _cat([q_lat, kv_lat, k_rope, jnp.zeros_like(k_rope)])


def _proj_body(epis, names, cast, h_ref, w_ref, *refs):
    n = len(names)
    ex = dict(zip(names, refs[:n]))
    o_ref = refs[n]
    j, i = pl.program_id(0), pl.program_id(1)
    if cast:
        wb_ref = refs[n + 1]

        @pl.when(i == 0)
        def _():
            wb_ref[...] = w_ref[...].astype(BF16)

        w = wb_ref[...]
    else:
        w = w_ref[...]
    acc = jnp.dot(h_ref[...], w, preferred_element_type=F32)
    if len(epis) == 1:
        o_ref[...] = epis[0][2](acc, ex).astype(o_ref.dtype)
    else:
        for t0, t1, epi in epis:
            @pl.when((j >= t0) & (j < t1))
            def _(epi=epi):
                o_ref[...] = epi(acc, ex).astype(o_ref.dtype)


def proj(h, w, epis, extra=(), *, h_block=0, w_tile0=0, n_tiles=None, tn=PROJ_TN, out_dtype=BF16, name="proj"):
    S = h.shape[0]
    K = w.shape[0]
    tm = PROJ_TM
    n_tiles = (w.shape[1] // tn - w_tile0) if n_tiles is None else n_tiles
    cast = w.dtype != BF16
    in_specs = [pl.BlockSpec((tm, K), lambda j, i: (i, h_block)),
                pl.BlockSpec((K, tn), lambda j, i: (0, w_tile0 + j))]
    args = [h, w]
    names = []
    for nm, arr, kind in extra:
        names.append(nm)
        if kind == "vec":
            in_specs.append(pl.BlockSpec(arr.shape, lambda j, i: (0, 0)))
        else:
            in_specs.append(pl.BlockSpec((tm, LANES), lambda j, i: (i, 0)))
        args.append(arr)
    return pl.pallas_call(
        functools.partial(_proj_body, tuple(epis), tuple(names), cast),
        out_shape=jax.ShapeDtypeStruct((S, n_tiles * tn), out_dtype),
        grid=(n_tiles, S // tm),
        in_specs=in_specs,
        out_specs=pl.BlockSpec((tm, tn), lambda j, i: (i, j)),
        scratch_shapes=[pltpu.VMEM((K, tn), BF16)] if cast else [],
        compiler_params=_params("arbitrary", "arbitrary"),
        name=name,
    )(*args)


def _outproj_body(n_a, has_next, *refs):
    a_refs = refs[:n_a]
    w_ref, x_ref = refs[n_a], refs[n_a + 1]
    if has_next:
        g_ref, xo_ref, ho_ref = refs[n_a + 2:n_a + 5]
    else:
        xo_ref = refs[n_a + 2]
    if n_a == 1:
        a = a_refs[0][...]
    else:
        a = a_refs[0][...].astype(F32)
        for r in a_refs[1:]:
            a = a + r[...].astype(F32)
        a = a.astype(BF16)
    xn = x_ref[...] + jnp.dot(a, w_ref[...], preferred_element_type=F32)
    xo_ref[...] = xn
    if has_next:
        ms = jnp.mean(xn * xn, axis=-1, keepdims=True)
        ho_ref[...] = (xn * lax.rsqrt(ms + EPS) * g_ref[...]).astype(BF16)


def outproj(a_list, w, x, g_next=None, tm=256):
    S, D = x.shape
    n_a = len(a_list)
    has_next = g_next is not None
    row = pl.BlockSpec((tm, D), lambda i: (i, 0))
    in_specs = [pl.BlockSpec((tm, MIX_WIDTH), lambda i: (i, 0)) for _ in a_list]
    in_specs += [pl.BlockSpec((MIX_WIDTH, D), lambda i: (0, 0)), row]
    args = list(a_list) + [w, x]
    out_shape = [jax.ShapeDtypeStruct((S, D), F32)]
    out_specs = [row]
    if has_next:
        in_specs.append(pl.BlockSpec((1, D), lambda i: (0, 0)))
        args.append(g_next.reshape(1, D))
        out_shape.append(jax.ShapeDtypeStruct((S, D), BF16))
        out_specs.append(row)
    res = pl.pallas_call(
        functools.partial(_outproj_body, n_a, has_next),
        out_shape=out_shape,
        grid=(S // tm,),
        in_specs=in_specs,
        out_specs=out_specs,
        compiler_params=_params("parallel"),
        name="outproj",
    )(*args)
    return (res[0], res[1]) if has_next else (res[0], None)


def _lane_col(x, c):
    lane = lax.broadcasted_iota(jnp.int32, x.shape, 1)
    return jnp.sum(jnp.where(lane == c, x, 0.0), axis=-1, keepdims=True)


def _head_lanes(n):
    return slice(n * LANES, (n + 1) * LANES)


def _flash_body(T, NH, dq, k_modes, v_mode, qx_mode, has_alibi, og_row0, *refs):
    pos = 0
    q_ref = refs[pos]; pos += 1
    if qx_mode is not None:
        qx_ref = refs[pos]; pos += 1
    k_refs = refs[pos:pos + len(k_modes)]; pos += len(k_modes)
    v_ref = refs[pos]; pos += 1
    if has_alibi:
        al_ref, sl_ref = refs[pos], refs[pos + 1]; pos += 2
    omul_ref = refs[pos]; pos += 1
    if og_row0 is not None:
        og_ref = refs[pos]; pos += 1
    o_ref = refs[pos]

    i = pl.program_id(1)
    hs = range(NH)
    qs = []
    for n in hs:
        qn = q_ref[:, n * dq:(n + 1) * dq]
        if qx_mode == "head":
            qn = jnp.concatenate([qn, qx_ref[n]], axis=-1)
        elif qx_mode == "group":
            qn = jnp.concatenate([qn, qx_ref[...]], axis=-1)
        qs.append(qn)
    dn = (((1,), (1,)), ((), ()))

    def tile(j, carry, masked):
        start = pl.multiple_of(j * T, T)
        parts = [kr[pl.ds(start, T), :] for kr in k_refs]
        ks = [_cat([p[:, _head_lanes(n)] if mode == "head" else p for p, mode in zip(parts, k_modes)])
              for n in hs]
        ss = [lax.dot_general(qs[n], ks[n], dn, preferred_element_type=F32) for n in hs]
        if has_alibi:
            off = ((j - i) * T).astype(F32)
            ss = [ss[n] + (al_ref[n] + sl_ref[n] * off) for n in hs]
        if masked:
            rows = lax.broadcasted_iota(jnp.int32, (T, T), 0)
            cols = lax.broadcasted_iota(jnp.int32, (T, T), 1)
            ss = [jnp.where(rows >= cols, s, NEG_BIG) for s in ss]
        ms = [jnp.maximum(carry[n][0], jnp.max(ss[n], axis=-1, keepdims=True)) for n in hs]
        alphas = [jnp.exp2(carry[n][0] - ms[n]) for n in hs]
        ps = [jnp.exp2(ss[n] - ms[n]) for n in hs]
        ls = [alphas[n] * carry[n][1] + jnp.sum(ps[n], axis=-1, keepdims=True) for n in hs]
        v = v_ref[pl.ds(start, T), :]
        accs = [alphas[n] * carry[n][2]
                + jnp.dot(ps[n].astype(BF16), v[:, _head_lanes(n)] if v_mode == "head" else v,
                          preferred_element_type=F32) for n in hs]
        return tuple(zip(ms, ls, accs))

    init = tuple((jnp.full((T, 1), NEG_BIG, F32), jnp.zeros((T, 1), F32), jnp.zeros((T, LANES), F32))
                 for _ in hs)
    carry = tile(i, init, True)
    carry = lax.fori_loop(0, i, lambda j, c: tile(j, c, False), carry)
    outs = []
    for n in hs:
        o = carry[n][2] / carry[n][1] * omul_ref[:, _head_lanes(n)].astype(F32)
        if og_row0 is not None:
            o = o * _lane_col(og_ref[...], og_row0 + pl.program_id(0) * NH + n)
        outs.append(o.astype(o_ref.dtype))
    o_ref[...] = _cat(outs)


def _slab_spec(rows, src, head_of, width=LANES):
    off = src[1] * LANES // width
    if rows is None:
        return pl.BlockSpec((src[0].shape[0], width), lambda h, i: (0, off + head_of(h)))
    return pl.BlockSpec((rows, width), lambda h, i: (i, off + head_of(h)))


def _same(h):
    return h


def _first(h):
    return 0


def _shared_spec(src, mode, NH, group):
    if mode == "head":
        return _slab_spec(None, src, _same, NH * LANES)
    if mode == "group":
        return _slab_spec(None, src, lambda hp: (hp * NH) // group)
    return _slab_spec(None, src, _first)


def _alibi_rows(n):
    sl = _alibi_slopes()[:, None, None] * LOG2E
    al = (sl * np.arange(n)[None, None, :]).astype(np.float32)
    return jnp.asarray(al), jnp.asarray(np.broadcast_to(sl, (N_HEADS, 1, n)).astype(np.float32))


def flash(q, k_parts, v, omul, *, dq, T, NH=FLASH_HEADS, group=1, qx=None, alibi=False,
          og=None, og_row0=0, name="flash"):
    S = q[0].shape[0]
    H = N_HEADS
    assert H % NH == 0 and (group == 1 or group % NH == 0)
    in_specs = [_slab_spec(T, q, _same, NH * dq)]
    args = [q[0]]
    if qx is not None:
        if qx[1] == "head":
            in_specs.append(pl.BlockSpec((NH, T, LANES), lambda hp, i: (hp, i, 0)))
        else:
            in_specs.append(pl.BlockSpec((None, T, LANES), lambda hp, i: ((hp * NH) // group, i, 0)))
        args.append(qx[0])
    for src, mode in k_parts:
        in_specs.append(_shared_spec(src, mode, NH, group))
        args.append(src[0])
    in_specs.append(_shared_spec(v[0], v[1], NH, group))
    args.append(v[0][0])
    if alibi:
        in_specs += [pl.BlockSpec((NH, 1, T), lambda hp, i: (hp, 0, 0))] * 2
        args += list(_alibi_rows(T))
    in_specs.append(_slab_spec(T, omul, _same, NH * LANES))
    args.append(omul[0])
    if og is not None:
        in_specs.append(pl.BlockSpec((T, LANES), lambda hp, i: (i, 0)))
        args.append(og)
    body = functools.partial(_flash_body, T, NH, dq, tuple(m for _, m in k_parts), v[1],
                             None if qx is None else qx[1], alibi, None if og is None else og_row0)
    return pl.pallas_call(
        body,
        out_shape=jax.ShapeDtypeStruct((S, H * LANES), BF16),
        grid=(H // NH, S // T),
        in_specs=in_specs,
        out_specs=pl.BlockSpec((T, NH * LANES), lambda hp, i: (i, hp)),
        compiler_params=_params("parallel", "arbitrary"),
        name=name,
    )(*args)


def _window_body(T, W, NH, og_row0, q_ref, k_ref, v_ref, al_ref, sl_ref, omul_ref, og_ref, o_ref):
    i = pl.program_id(1)
    hs = range(NH)
    span = W + T
    q0 = i * T
    start = pl.multiple_of(jnp.maximum(q0 - W, 0), T)
    k = k_ref[pl.ds(start, span), :]
    dn = (((1,), (1,)), ((), ()))
    ss = [lax.dot_general(q_ref[:, _head_lanes(n)], k, dn, preferred_element_type=F32) for n in hs]
    off = (start - q0).astype(F32)
    rows = lax.broadcasted_iota(jnp.int32, (T, span), 0)
    cols = lax.broadcasted_iota(jnp.int32, (T, span), 1)
    dist = rows - cols + (q0 - start)
    ok = (dist >= 0) & (dist < W)
    ss = [jnp.where(ok, ss[n] + (al_ref[n] + sl_ref[n] * off), NEG_BIG) for n in hs]
    ps = [jnp.exp2(s - jnp.max(s, axis=-1, keepdims=True)) for s in ss]
    ls = [jnp.sum(p, axis=-1, keepdims=True) for p in ps]
    v = v_ref[pl.ds(start, span), :]
    os_ = [jnp.dot(p.astype(BF16), v, preferred_element_type=F32) for p in ps]
    og = og_ref[...]
    outs = [(os_[n] / ls[n] * omul_ref[:, _head_lanes(n)].astype(F32)
             * _lane_col(og, og_row0 + pl.program_id(0) * NH + n)).astype(o_ref.dtype) for n in hs]
    o_ref[...] = _cat(outs)


def window_attention(q, k, v, omul, og, og_row0, *, group, T, NH=2, W=NSA_WINDOW):
    S = q[0].shape[0]
    span = W + T
    assert W % T == 0 and S >= span and group % NH == 0
    al, sl = _alibi_rows(span)
    vec = pl.BlockSpec((NH, 1, span), lambda hp, i: (hp, 0, 0))
    return pl.pallas_call(
        functools.partial(_window_body, T, W, NH, og_row0),
        out_shape=jax.ShapeDtypeStruct((S, N_HEADS * LANES), BF16),
        grid=(N_HEADS // NH, S // T),
        in_specs=[_slab_spec(T, q, _same, NH * LANES), _shared_spec(k, "group", NH, group),
                  _shared_spec(v, "group", NH, group), vec, vec, _slab_spec(T, omul, _same, NH * LANES),
                  pl.BlockSpec((T, LANES), lambda hp, i: (i, 0))],
        out_specs=pl.BlockSpec((T, NH * LANES), lambda hp, i: (i, hp)),
        compiler_params=_params("parallel", "arbitrary"),
        name="nsa_window",
    )(q[0], k[0], v[0], al, sl, omul[0], og)


def _block_onehot(S, block):
    e = (np.arange(S)[:, None] // block) == np.arange(LANES)[None, :]
    return jnp.asarray(e.astype(np.float32), dtype=BF16)


def _take_top(score, col, n):
    colf = col.astype(F32)
    left = score
    for _ in range(n):
        m = jnp.max(left, axis=-1, keepdims=True)
        idx = jnp.min(jnp.where(left == m, colf, float(LANES)), axis=-1, keepdims=True)
        left = jnp.where(colf == idx, -jnp.inf, left)
    return (left == -jnp.inf) & (score > -jnp.inf)


def _moba_select_body(T, NB, q_ref, k_ref, o_ref, km_ref):
    i = pl.program_id(1)

    @pl.when(i == 0)
    def _():
        km_ref[...] = jnp.zeros_like(km_ref)
        for b in range(NB):
            kb = k_ref[b * MOBA_BLOCK:(b + 1) * MOBA_BLOCK, :].astype(F32)
            km_ref[b:b + 1, :] = jnp.sum(kb, axis=0, keepdims=True) * (1.0 / MOBA_BLOCK)

    q = q_ref[...]
    km = km_ref[...]
    km_hi = km.astype(BF16)
    km_lo = (km - km_hi.astype(F32)).astype(BF16)
    dn = (((1,), (1,)), ((), ()))
    score = (lax.dot_general(q, km_hi, dn, preferred_element_type=F32)
             + lax.dot_general(q, km_lo, dn, preferred_element_type=F32))
    col = lax.broadcasted_iota(jnp.int32, (T, LANES), 1)
    row = lax.broadcasted_iota(jnp.int32, (T, LANES), 0)
    q_blk = (i * T + row) // MOBA_BLOCK
    score = jnp.where(col < q_blk, score, -jnp.inf)
    taken = _take_top(score, col, min(MOBA_TOPK, NB)) | (col == q_blk)
    o_ref[...] = jnp.where(taken, 0.0, NEG_BIG).astype(o_ref.dtype)


def moba_select(q, k, T=512):
    S = q[0].shape[0]
    NB = S // MOBA_BLOCK
    return pl.pallas_call(
        functools.partial(_moba_select_body, T, NB),
        out_shape=jax.ShapeDtypeStruct((N_HEADS, S, LANES), BF16),
        grid=(N_HEADS, S // T),
        in_specs=[_slab_spec(T, q, _same), _slab_spec(None, k, _same)],
        out_specs=pl.BlockSpec((None, T, LANES), lambda h, i: (h, i, 0)),
        scratch_shapes=[pltpu.VMEM((LANES, LANES), F32)],
        compiler_params=_params("parallel", "arbitrary"),
        name="moba_select",
    )(q[0], k[0])


def _softplus(z):
    return jnp.maximum(z, 0.0) + jnp.log(1.0 + jnp.exp(-jnp.abs(z)))


def _stick_body(T, NH, q_ref, k_ref, v_ref, omul_ref, o_ref):
    i = pl.program_id(1)
    rows = lax.broadcasted_iota(jnp.int32, (T, T), 0)
    cols = lax.broadcasted_iota(jnp.int32, (T, T), 1)
    later = jnp.where(rows > cols, 1.0, 0.0).astype(BF16)
    before = cols < rows
    hs = range(NH)
    qs = [q_ref[:, _head_lanes(n)] for n in hs]
    dn = (((1,), (1,)), ((), ()))

    def tiles(j, state, diag):
        start = pl.multiple_of(j * T, T)
        zs = [lax.dot_general(qs[n], k_ref[pl.ds(start, T), _head_lanes(n)], dn,
                              preferred_element_type=F32) for n in hs]
        log_keeps = [-_softplus(z) for z in zs]
        if diag:
            log_keeps = [jnp.where(before, lk, 0.0) for lk in log_keeps]
        his = [lk.astype(BF16) for lk in log_keeps]
        los = [(lk - hi.astype(F32)).astype(BF16) for lk, hi in zip(log_keeps, his)]
        b_his = [jnp.dot(hi, later, preferred_element_type=F32) for hi in his]
        b_los = [jnp.dot(lo, later, preferred_element_type=F32) for lo in los]
        aa = [jnp.exp(zs[n] + log_keeps[n] + (b_his[n] + b_los[n]) + state[n][0]) for n in hs]
        if diag:
            aa = [jnp.where(before, a, 0.0) for a in aa]
        accs = [state[n][1] + jnp.dot(aa[n].astype(BF16), v_ref[pl.ds(start, T), _head_lanes(n)],
                                      preferred_element_type=F32) for n in hs]
        runs = [state[n][0] + jnp.sum(log_keeps[n], axis=-1, keepdims=True) for n in hs]
        return tuple(zip(runs, accs))

    state = tiles(i, tuple((jnp.zeros((T, 1), F32), jnp.zeros((T, LANES), F32)) for _ in hs), True)

    def cond(st):
        j, state = st
        top = state[0][0]
        for run, _ in state[1:]:
            top = jnp.maximum(top, run)
        return jnp.logical_and(j >= 0, jnp.max(top) > SB_LOG_CUTOFF)

    def step(st):
        j, state = st
        return j - 1, tiles(j, state, False)

    _, state = lax.while_loop(cond, step, (i - 1, state))
    o = _cat([acc for _, acc in state])
    o_ref[...] = (o * omul_ref[...].astype(F32)).astype(o_ref.dtype)


def stick_attention(q, k, v, omul, T, NH=STICK_HEADS):
    S = q[0].shape[0]
    width = NH * LANES
    grid = (N_HEADS // NH, S // T)
    return pl.pallas_call(
        functools.partial(_stick_body, T, NH),
        out_shape=jax.ShapeDtypeStruct((S, MIX_WIDTH), BF16),
        grid=grid,
        in_specs=[_slab_spec(T, q, _same, width), _slab_spec(None, k, _same, width),
                  _slab_spec(None, v, _same, width), _slab_spec(T, omul, _same, width)],
        out_specs=pl.BlockSpec((T, width), lambda h, i: (i, h)),
        compiler_params=_params("parallel", "arbitrary"),
        name="stick_attention",
    )(q[0], k[0], v[0], omul[0])


def _nsa_compress_body(norm, x_ref, w_ref, pos_ref, g_ref, o_ref):
    half = NSA_CMP_STRIDE * HEAD_DIM
    x = x_ref[...]
    first = jnp.dot(x, w_ref[:half, :], preferred_element_type=F32)
    second = jnp.dot(x, w_ref[half:, :], preferred_element_type=F32)
    bias = jnp.dot(pos_ref[...], w_ref[...], preferred_element_type=F32)
    n = x.shape[0]
    y = first + pltpu.roll(second, n - 1, axis=0) + bias[0:1, :]
    if norm:
        y = _head_rms(y, g_ref[...], HEAD_DIM)
    o_ref[...] = y.astype(o_ref.dtype)


def nsa_compress(x, w, pos, gain, norm):
    G, n, width = x.shape
    pos_flat = jnp.broadcast_to(pos.reshape(1, -1), (16, pos.size)).astype(BF16)
    return pl.pallas_call(
        functools.partial(_nsa_compress_body, norm),
        out_shape=jax.ShapeDtypeStruct((G, n, HEAD_DIM), BF16),
        grid=(G,),
        in_specs=[pl.BlockSpec((None, n, width), lambda g: (g, 0, 0)),
                  pl.BlockSpec(w.shape, lambda g: (0, 0)),
                  pl.BlockSpec((16, pos.size), lambda g: (0, 0)),
                  pl.BlockSpec((1, HEAD_DIM), lambda g: (0, 0))],
        out_specs=pl.BlockSpec((None, n, HEAD_DIM), lambda g: (g, 0, 0)),
        compiler_params=_params("parallel"),
        name="nsa_compress",
    )(x, w.astype(BF16), pos_flat, gain.reshape(1, HEAD_DIM))


def _nsa_cmp_body(T, n_cmp, n_sel, q_ref, kc_ref, vc_ref, c2s_ref, sl_ref, omul_ref, og_ref,
                  o_ref, sel_ref):
    g, i = pl.program_id(0), pl.program_id(1)
    NC = kc_ref.shape[0]
    kc = kc_ref[...]
    c2s = c2s_ref[...]
    vc_c2s = jnp.concatenate([vc_ref[...], c2s], axis=-1)
    og = og_ref[...]
    row = lax.broadcasted_iota(jnp.int32, (T, NC), 0)
    col = lax.broadcasted_iota(jnp.int32, (T, NC), 1)
    dist = i * T + row - (col * NSA_CMP_STRIDE + (NSA_CMP_BLOCK - 1))
    ok = (dist >= 0) & (col < n_cmp)
    dist_f = dist.astype(F32)
    imp = jnp.zeros((T, LANES), F32)
    outs = []
    for p in range(NSA_HPG):
        qp = q_ref[:, p * LANES:(p + 1) * LANES]
        s = lax.dot_general(qp, kc, (((1,), (1,)), ((), ())), preferred_element_type=F32)
        s = jnp.where(ok, s - sl_ref[p] * dist_f, NEG_BIG)
        m = jnp.max(s, axis=-1, keepdims=True)
        e = jnp.exp2(s - m)
        l = jnp.sum(e, axis=-1, keepdims=True)
        inv = jnp.where(m > 0.5 * NEG_BIG, 1.0 / l, 0.0)
        prob = e * inv
        hi = prob.astype(BF16)
        lo = (prob - hi.astype(F32)).astype(BF16)
        both = jnp.dot(hi, vc_c2s, preferred_element_type=F32)
        o = both[:, :LANES]
        imp = imp + (both[:, LANES:] + jnp.dot(lo, c2s, preferred_element_type=F32))
        o = o * _lane_col(og, g * NSA_HPG + p) * omul_ref[:, p * LANES:(p + 1) * LANES].astype(F32)
        outs.append(o.astype(o_ref.dtype))
    o_ref[...] = jnp.concatenate(outs, axis=-1)

    blk = lax.broadcasted_iota(jnp.int32, (T, LANES), 1)
    cur = (i * T + lax.broadcasted_iota(jnp.int32, (T, LANES), 0)) // NSA_SEL_BLOCK
    kn = min(NSA_SEL_N, n_sel)
    forced = ((blk == 0) | (blk == cur) | (blk == cur - 1)) & (blk <= cur)
    imp = jnp.where((blk > cur) | (blk >= n_sel) | forced, -jnp.inf, imp)
    taken = forced | _take_top(imp, blk, kn - 3)
    sel_ref[...] = jnp.where(taken, 0.0, NEG_BIG).astype(sel_ref.dtype)


def nsa_compressed(q, k_cmp, v_cmp, omul, og, T):
    S = q[0].shape[0]
    G, NC, _ = k_cmp.shape
    n_cmp = (S - NSA_CMP_BLOCK) // NSA_CMP_STRIDE + 1
    n_sel = S // NSA_SEL_BLOCK
    assert 3 <= n_sel <= LANES and NC >= n_cmp
    c_start = np.arange(NC)[:, None] * NSA_CMP_STRIDE
    s_start = np.arange(LANES)[None, :] * NSA_SEL_BLOCK
    c2s = ((c_start < s_start + NSA_SEL_BLOCK) & (c_start + NSA_CMP_BLOCK > s_start)
           & (np.arange(NC)[:, None] < n_cmp) & (np.arange(LANES)[None, :] < n_sel))
    slopes = np.broadcast_to(_alibi_slopes()[:, None, None] * LOG2E, (N_HEADS, 1, NC)).astype(np.float32)
    gw = NSA_HPG * LANES
    cmp_spec = pl.BlockSpec((None, NC, HEAD_DIM), lambda g, i: (g, 0, 0))
    return pl.pallas_call(
        functools.partial(_nsa_cmp_body, T, n_cmp, n_sel),
        out_shape=[jax.ShapeDtypeStruct((S, MIX_WIDTH), BF16),
                   jax.ShapeDtypeStruct((G, S, LANES), BF16)],
        grid=(G, S // T),
        in_specs=[_slab_spec(T, q, _same, gw), cmp_spec, cmp_spec,
                  pl.BlockSpec((NC, LANES), lambda g, i: (0, 0)),
                  pl.BlockSpec((NSA_HPG, 1, NC), lambda g, i: (g, 0, 0)),
                  _slab_spec(T, omul, _same, gw),
                  pl.BlockSpec((T, LANES), lambda g, i: (i, 0))],
        out_specs=[pl.BlockSpec((T, gw), lambda g, i: (i, g)),
                   pl.BlockSpec((None, T, LANES), lambda g, i: (g, i, 0))],
        compiler_params=_params("parallel", "parallel"),
        name="nsa_compressed",
    )(q[0], k_cmp, v_cmp, jnp.asarray(c2s.astype(np.float32), dtype=BF16), jnp.asarray(slopes),
      omul[0], og)


def _row(v):
    return v.reshape(1, -1).astype(F32)


def _rope_tables(S):
    inv_freq = ROPE_THETA ** (-jnp.arange(0, MLA_ROPE, 2, dtype=F32) / MLA_ROPE)
    ang = jnp.arange(S, dtype=F32)[:, None] * inv_freq[None, :]
    cos, sin = jnp.cos(ang), jnp.sin(ang)
    z = jnp.zeros_like(cos)
    return (jnp.concatenate([cos, z, cos, z], axis=-1),
            jnp.concatenate([-sin, z, sin, z], axis=-1))


def _rope_lanes(v):
    half = MLA_ROPE // 2
    z = jnp.zeros(v.shape[:-1] + (half,), v.dtype)
    return jnp.concatenate([v[..., :half], z, v[..., half:], z], axis=-1)


def mla_mixer(h, w_in, q_lat_norm, w_q_up, kv_lat_norm, w_kv_up, q_norm, k_norm):
    S = h.shape[0]
    c1 = MLA_Q_RANK
    c2 = c1 + MLA_KV_RANK
    c3 = c2 + MLA_ROPE
    scale = MLA_QK_DIM ** -0.5 * LOG2E
    cos, sin = _rope_tables(S)
    tables = [("cos", cos, "row"), ("sin", sin, "row")]

    w_lat = jnp.concatenate([w_in[:, :c2], _rope_lanes(w_in[:, c2:c3]),
                             jnp.zeros((w_in.shape[0], PROJ_TN - c2 - LANES), w_in.dtype)], axis=-1)
    lat = proj(h, w_lat, [(0, 1, _epi_mla_in)],
               [("g_qlat", _row(q_lat_norm), "vec"), ("g_kvlat", _row(kv_lat_norm), "vec"),
                ("gk_r", _row(_rope_lanes(k_norm[MLA_NOPE:])), "vec")] + tables, name="mla_latents")
    gate = proj(h, w_in[:, c3:], [(0, 2, _epi_silu)], name="mla_gate")

    wq = w_q_up.reshape(MLA_Q_RANK, N_HEADS, MLA_QK_DIM)
    wq = jnp.concatenate([wq[..., :MLA_NOPE], _rope_lanes(wq[..., MLA_NOPE:])], axis=-1)
    q = proj(lat, wq.reshape(MLA_Q_RANK, N_HEADS * 2 * LANES).astype(BF16), [(0, 4, _epi_mla_q(scale))],
             [("gq_n", _row(q_norm[:MLA_NOPE]), "vec"), ("gq_r", _row(_rope_lanes(q_norm[MLA_NOPE:])), "vec")]
             + tables, name="mla_q")
    wkv = w_kv_up.reshape(MLA_KV_RANK, N_HEADS, MLA_NOPE + HEAD_DIM)
    wkv = jnp.concatenate([wkv[..., :MLA_NOPE].reshape(MLA_KV_RANK, -1),
                           wkv[..., MLA_NOPE:].reshape(MLA_KV_RANK, -1)], axis=-1).astype(BF16)
    kv = proj(lat, wkv, [(0, 2, _epi_headnorm("gk_n")), (2, 4, _epi_scale(1.0))],
              [("gk_n", _row(k_norm[:MLA_NOPE]), "vec")], h_block=c1 // MLA_KV_RANK, name="mla_kv")
    k_rope = (lat, c2 // LANES)
    return [flash((q, 0), [((kv, 0), "head"), (k_rope, "all")], ((kv, N_HEADS), "head"), (gate, 0),
                  dq=2 * LANES, T=min(FLASH_T, S), name="mla_attention")]


def moba_mixer(h, w_in, q_norm, k_norm):
    S = h.shape[0]
    scale = HEAD_DIM ** -0.5 * LOG2E
    n = MIX_WIDTH // PROJ_TN
    qkvg = proj(h, w_in, [(0, n, _epi_headnorm("gq", scale)), (n, 2 * n, _epi_headnorm("gk")),
                          (2 * n, 3 * n, _epi_scale(1.0)), (3 * n, 4 * n, _epi_silu)],
                [("gq", _row(q_norm), "vec"), ("gk", _row(k_norm), "vec")], name="moba_in")
    H = N_HEADS
    q, k, v, gate = (qkvg, 0), (qkvg, H), (qkvg, 2 * H), (qkvg, 3 * H)
    bias = moba_select(q, k)
    return [flash(q, [(k, "head"), ((_block_onehot(S, MOBA_BLOCK), 0), "all")], (v, "head"), gate, dq=LANES,
                  qx=(bias, "head"), alibi=True, T=min(FLASH_T, S), name="moba_attention")]


def stick_mixer(h, w_in):
    S = h.shape[0]
    scale = HEAD_DIM ** -0.5
    n = MIX_WIDTH // PROJ_TN
    qkvg = proj(h, w_in, [(0, n, _epi_scale(scale)), (n, 3 * n, _epi_scale(1.0)), (3 * n, 4 * n, _epi_silu)],
                name="stick_in")
    H = N_HEADS
    return [stick_attention((qkvg, 0), (qkvg, H), (qkvg, 2 * H), (qkvg, 3 * H), T=min(STICK_T, S))]


def nsa_mixer(h, w_in, q_norm, k_norm, w_cmp_k, w_cmp_v, cmp_pos):
    S = h.shape[0]
    G, P, H = NSA_GROUPS, NSA_HPG, N_HEADS
    W, KV = MIX_WIDTH, NSA_KV_WIDTH
    half = KV // 2
    scale = HEAD_DIM ** -0.5 * LOG2E
    cuts = np.cumsum([W, KV, KV, KV, 3 * H]).tolist()
    assert KV == PROJ_TN and W % PROJ_TN == 0
    n = W // PROJ_TN
    main = proj(h, w_in, [(0, n, _epi_headnorm("gq", scale)), (n, n + 1, _epi_scale(1.0)),
                          (n + 1, n + 2, _epi_headnorm("gk1", n_norm=half)),
                          (n + 2, n + 3, _epi_headnorm("gk2", n_norm=half))],
                [("gq", _row(q_norm), "vec"), ("gk1", _row(k_norm[1]), "vec"), ("gk2", _row(k_norm[2]), "vec")],
                n_tiles=n + 3, name="nsa_in")
    q = (main, 0)
    kc0 = H
    ks, vs = (main, kc0 + 2 * G), (main, kc0 + 3 * G)
    kw, vw = (main, kc0 + 4 * G), (main, kc0 + 5 * G)
    wg = w_in[:, cuts[3]:cuts[4]].reshape(-1, H, 3).transpose(0, 2, 1).reshape(-1, 3 * H)
    wg = jnp.pad(wg, ((0, 0), (0, LANES - 3 * H)))
    og = proj(h, wg.astype(BF16), [(0, 1, _epi_sigmoid)], tn=LANES, out_dtype=F32, name="nsa_branch_gates")
    gate = (proj(h, w_in[:, cuts[4]:], [(0, n, _epi_silu)], name="nsa_gate"), 0)

    def chunks(t):
        return t.reshape(S // NSA_CMP_STRIDE, NSA_CMP_STRIDE, G, HEAD_DIM).transpose(2, 0, 1, 3).reshape(
            G, S // NSA_CMP_STRIDE, NSA_CMP_STRIDE * HEAD_DIM)

    c0 = kc0 * LANES
    k_cmp = nsa_compress(chunks(main[:, c0:c0 + half]), w_cmp_k, cmp_pos, k_norm[0], True)
    v_cmp = nsa_compress(chunks(main[:, c0 + half:c0 + KV]), w_cmp_v, cmp_pos, k_norm[0], False)
    o_cmp, bias = nsa_compressed(q, k_cmp, v_cmp, gate, og, T=min(NSA_CMP_T, S))
    o_slc = flash(q, [(ks, "group"), ((_block_onehot(S, NSA_SEL_BLOCK), 0), "all")], (vs, "group"), gate,
                  dq=LANES, group=P, qx=(bias, "group"), alibi=True, og=og, og_row0=H,
                  T=min(FLASH_T, S), name="nsa_selected")
    o_win = window_attention(q, kw, vw, gate, og, 2 * H, group=P, T=min(WINDOW_T, S))
    return [o_cmp, o_slc, o_win]


def kernel(x, norm_a, w_in_a, q_lat_norm_a, w_q_up_a, kv_lat_norm_a, w_kv_up_a, q_norm_a, k_norm_a, w_out_a,
           norm_b, w_in_b, q_norm_b, k_norm_b, w_out_b,
           norm_c, w_in_c, w_out_c,
           norm_d, w_in_d, q_norm_d, k_norm_d, w_cmp_k_d, w_cmp_v_d, cmp_pos_d, w_out_d):
    B, S, D = x.shape
    norms = (norm_a, norm_b, norm_c, norm_d)
    w_outs = (w_out_a, w_out_b, w_out_c, w_out_d)
    depth = norm_a.shape[0] + norm_b.shape[0] + norm_c.shape[0] + norm_d.shape[0]
    rows = x.reshape(B * S, D)
    outs = []
    for b in range(B):
        xb = rows if B == 1 else rows[b * S:(b + 1) * S]
        h = rmsnorm_rows(xb, norms[0][0])
        for layer in range(depth):
            m, j = layer % 4, layer // 4
            if m == 0:
                a = mla_mixer(h, w_in_a[j], q_lat_norm_a[j], w_q_up_a[j], kv_lat_norm_a[j], w_kv_up_a[j],
                              q_norm_a[j], k_norm_a[j])
            elif m == 1:
                a = moba_mixer(h, w_in_b[j], q_norm_b[j], k_norm_b[j])
            elif m == 2:
                a = stick_mixer(h, w_in_c[j])
            else:
                a = nsa_mixer(h, w_in_d[j], q_norm_d[j], k_norm_d[j], w_cmp_k_d[j], w_cmp_v_d[j], cmp_pos_d[j])
            nxt = layer + 1
            g_next = norms[nxt % 4][nxt // 4] if nxt < depth else None
            xb, h = outproj(a, w_outs[m][j].astype(BF16), xb, g_next)
        outs.append(xb)
    out = outs[0] if B == 1 else jnp.concatenate(outs, axis=0)
    return out.reshape(B, S, D)
```

```python
import functools

import numpy as np
import jax
import jax.numpy as jnp
from jax import lax
from jax.experimental import pallas as pl
from jax.experimental.pallas import tpu as pltpu

N_HEADS = 16
HEAD_DIM = 128
MIX_WIDTH = N_HEADS * HEAD_DIM
EPS = 1e-6
NEG_BIG = -1e30
LANES = 128

MLA_Q_RANK = 512
MLA_KV_RANK = 256
MLA_NOPE = 128
MLA_ROPE = 64
MLA_QK_DIM = MLA_NOPE + MLA_ROPE
ROPE_THETA = 10000.0

MOBA_BLOCK = 256
MOBA_TOPK = 3

NSA_GROUPS = 4
NSA_HPG = N_HEADS // NSA_GROUPS
NSA_CMP_BLOCK = 32
NSA_CMP_STRIDE = 16
NSA_SEL_BLOCK = 64
NSA_SEL_N = 16
NSA_WINDOW = 512
NSA_KV_WIDTH = 2 * NSA_GROUPS * HEAD_DIM

SB_LOG_CUTOFF = -110.0
EXP2_ZERO = 151.0

VMEM_LIMIT = 56 * 1024 * 1024

LOG2E = 1.4426950408889634

FLASH_T = 1024
WINDOW_T = 512
STICK_T = 256
NSA_CMP_T = 512
FLASH_HEADS = 2
STICK_HEADS = 4
PROJ_TM = 512
PROJ_TN = 1024

BF16 = jnp.bfloat16
F32 = jnp.float32


def _params(*sem):
    return pltpu.CompilerParams(dimension_semantics=sem, vmem_limit_bytes=VMEM_LIMIT)


def _alibi_slopes():
    return 2.0 ** (-8.0 * np.arange(1, N_HEADS + 1) / N_HEADS)


def _rmsnorm_body(x_ref, g_ref, o_ref):
    x = x_ref[...]
    ms = jnp.mean(x * x, axis=-1, keepdims=True)
    o_ref[...] = (x * lax.rsqrt(ms + EPS) * g_ref[...]).astype(o_ref.dtype)


def rmsnorm_rows(x, g, tm=512):
    S, D = x.shape
    return pl.pallas_call(
        _rmsnorm_body,
        out_shape=jax.ShapeDtypeStruct((S, D), BF16),
        grid=(S // tm,),
        in_specs=[pl.BlockSpec((tm, D), lambda i: (i, 0)),
                  pl.BlockSpec((1, D), lambda i: (0, 0))],
        out_specs=pl.BlockSpec((tm, D), lambda i: (i, 0)),
        compiler_params=_params("parallel"),
        name="rmsnorm_rows",
    )(x, g.reshape(1, D))


def _head_rms(y, gain, n_real):
    ms = jnp.sum(y * y, axis=-1, keepdims=True) * (1.0 / n_real)
    return y * lax.rsqrt(ms + EPS) * gain


def _cat(parts):
    return parts[0] if len(parts) == 1 else jnp.concatenate(parts, axis=-1)


def _epi_scale(scale):
    def epi(acc, ex):
        return acc * scale if scale != 1.0 else acc
    return epi


def _epi_silu(acc, ex):
    return acc / (1.0 + jnp.exp(-acc))


def _epi_sigmoid(acc, ex):
    return 1.0 / (1.0 + jnp.exp(-acc))


def _epi_headnorm(gain, scale=1.0, n_norm=None):
    def epi(acc, ex):
        n = acc.shape[-1] if n_norm is None else n_norm
        g = ex[gain][...] * scale
        parts = [_head_rms(acc[:, c:c + LANES], g, LANES) for c in range(0, n, LANES)]
        if n < acc.shape[-1]:
            parts.append(acc[:, n:])
        return _cat(parts)
    return epi


def _rope_slab(y, gain, cos, sin, scale):
    y = _head_rms(y, gain, MLA_ROPE)
    return (y * cos + pltpu.roll(y, 64, axis=1) * sin) * scale


def _epi_mla_q(scale):
    def epi(acc, ex):
        gn = ex["gq_n"][...] * scale
        gr, cos, sin = ex["gq_r"][...], ex["cos"][...], ex["sin"][...]
        parts = []
        for c in range(0, acc.shape[-1], 2 * LANES):
            parts.append(_head_rms(acc[:, c:c + LANES], gn, MLA_NOPE))
            parts.append(_rope_slab(acc[:, c + LANES:c + 2 * LANES], gr, cos, sin, scale))
        return _cat(parts)
    return epi


def _epi_mla_in(acc, ex):
    c1, c2 = MLA_Q_RANK, MLA_Q_RANK + MLA_KV_RANK
    q_lat = _head_rms(acc[:, :c1], ex["g_qlat"][...], c1)
    kv_lat = _head_rms(acc[:, c1:c2], ex["g_kvlat"][...], MLA_KV_RANK)
    k_rope = _rope_slab(acc[:, c2:c2 + LANES], ex["gk_r"][...], ex["cos"][...], ex["sin"][...], 1.0)
    return _cat([q_lat, kv_lat, k_rope, jnp.zeros_like(k_rope)])


def _proj_body(epis, names, cast, h_ref, w_ref, *refs):
    n = len(names)
    ex = dict(zip(names, refs[:n]))
    o_ref = refs[n]
    j, i = pl.program_id(0), pl.program_id(1)
    if cast:
        wb_ref = refs[n + 1]

        @pl.when(i == 0)
        def _():
            wb_ref[...] = w_ref[...].astype(BF16)

        w = wb_ref[...]
    else:
        w = w_ref[...]
    acc = jnp.dot(h_ref[...], w, preferred_element_type=F32)
    if len(epis) == 1:
        o_ref[...] = epis[0][2](acc, ex).astype(o_ref.dtype)
    else:
        for t0, t1, epi in epis:
            @pl.when((j >= t0) & (j < t1))
            def _(epi=epi):
                o_ref[...] = epi(acc, ex).astype(o_ref.dtype)


def proj(h, w, epis, extra=(), *, h_block=0, w_tile0=0, n_tiles=None, tn=PROJ_TN, out_dtype=BF16, name="proj"):
    S = h.shape[0]
    K = w.shape[0]
    tm = PROJ_TM
    n_tiles = (w.shape[1] // tn - w_tile0) if n_tiles is None else n_tiles
    cast = w.dtype != BF16
    in_specs = [pl.BlockSpec((tm, K), lambda j, i: (i, h_block)),
                pl.BlockSpec((K, tn), lambda j, i: (0, w_tile0 + j))]
    args = [h, w]
    names = []
    for nm, arr, kind in extra:
        names.append(nm)
        if kind == "vec":
            in_specs.append(pl.BlockSpec(arr.shape, lambda j, i: (0, 0)))
        else:
            in_specs.append(pl.BlockSpec((tm, LANES), lambda j, i: (i, 0)))
        args.append(arr)
    return pl.pallas_call(
        functools.partial(_proj_body, tuple(epis), tuple(names), cast),
        out_shape=jax.ShapeDtypeStruct((S, n_tiles * tn), out_dtype),
        grid=(n_tiles, S // tm),
        in_specs=in_specs,
        out_specs=pl.BlockSpec((tm, tn), lambda j, i: (i, j)),
        scratch_shapes=[pltpu.VMEM((K, tn), BF16)] if cast else [],
        compiler_params=_params("arbitrary", "arbitrary"),
        name=name,
    )(*args)


def _outproj_body(n_a, has_next, *refs):
    a_refs = refs[:n_a]
    w_ref, x_ref = refs[n_a], refs[n_a + 1]
    if has_next:
        g_ref, xo_ref, ho_ref = refs[n_a + 2:n_a + 5]
    else:
        xo_ref = refs[n_a + 2]
    if n_a == 1:
        a = a_refs[0][...]
    else:
        a = a_refs[0][...].astype(F32)
        for r in a_refs[1:]:
            a = a + r[...].astype(F32)
        a = a.astype(BF16)
    xn = x_ref[...] + jnp.dot(a, w_ref[...], preferred_element_type=F32)
    xo_ref[...] = xn
    if has_next:
        ms = jnp.mean(xn * xn, axis=-1, keepdims=True)
        ho_ref[...] = (xn * lax.rsqrt(ms + EPS) * g_ref[...]).astype(BF16)


def outproj(a_list, w, x, g_next=None, tm=256):
    S, D = x.shape
    n_a = len(a_list)
    has_next = g_next is not None
    row = pl.BlockSpec((tm, D), lambda i: (i, 0))
    in_specs = [pl.BlockSpec((tm, MIX_WIDTH), lambda i: (i, 0)) for _ in a_list]
    in_specs += [pl.BlockSpec((MIX_WIDTH, D), lambda i: (0, 0)), row]
    args = list(a_list) + [w, x]
    out_shape = [jax.ShapeDtypeStruct((S, D), F32)]
    out_specs = [row]
    if has_next:
        in_specs.append(pl.BlockSpec((1, D), lambda i: (0, 0)))
        args.append(g_next.reshape(1, D))
        out_shape.append(jax.ShapeDtypeStruct((S, D), BF16))
        out_specs.append(row)
    res = pl.pallas_call(
        functools.partial(_outproj_body, n_a, has_next),
        out_shape=out_shape,
        grid=(S // tm,),
        in_specs=in_specs,
        out_specs=out_specs,
        compiler_params=_params("parallel"),
        name="outproj",
    )(*args)
    return (res[0], res[1]) if has_next else (res[0], None)


def _lane_col(x, c):
    lane = lax.broadcasted_iota(jnp.int32, x.shape, 1)
    return jnp.sum(jnp.where(lane == c, x, 0.0), axis=-1, keepdims=True)


def _head_lanes(n):
    return slice(n * LANES, (n + 1) * LANES)


def _flash_body(T, NH, dq, k_modes, v_mode, qx_mode, has_alibi, og_row0, *refs):
    pos = 0
    q_ref = refs[pos]; pos += 1
    if qx_mode is not None:
        qx_ref = refs[pos]; pos += 1
    k_refs = refs[pos:pos + len(k_modes)]; pos += len(k_modes)
    v_ref = refs[pos]; pos += 1
    if has_alibi:
        al_ref, sl_ref = refs[pos], refs[pos + 1]; pos += 2
    omul_ref = refs[pos]; pos += 1
    if og_row0 is not None:
        og_ref = refs[pos]; pos += 1
    o_ref = refs[pos]

    i = pl.program_id(1)
    hs = range(NH)
    if has_alibi:
        kmax_ref = refs[pos + 1]

        @pl.when(i == 0)
        def _():
            for n in hs:
                kn = k_refs[0][:, _head_lanes(n)] if k_modes[0] == "head" else k_refs[0][...]
                kn = kn.astype(F32)
                k2 = jnp.sum(kn * kn, axis=-1, keepdims=True)
                kmax_ref[n] = jnp.broadcast_to(jnp.max(k2, axis=0, keepdims=True), kmax_ref.shape[1:])

    qs = []
    for n in hs:
        qn = q_ref[:, n * dq:(n + 1) * dq]
        if qx_mode == "head":
            qn = jnp.concatenate([qn, qx_ref[n]], axis=-1)
        elif qx_mode == "group":
            qn = jnp.concatenate([qn, qx_ref[...]], axis=-1)
        qs.append(qn)
    dn = (((1,), (1,)), ((), ()))

    def keys(start):
        parts = [kr[pl.ds(start, T), :] for kr in k_refs]
        return [_cat([p[:, _head_lanes(n)] if mode == "head" else p for p, mode in zip(parts, k_modes)])
                for n in hs]

    def values(start):
        v = v_ref[pl.ds(start, T), :]
        return [v[:, _head_lanes(n)] if v_mode == "head" else v for n in hs]

    def tile(j, carry):
        start = pl.multiple_of(j * T, T)
        ks = keys(start)
        ss = [lax.dot_general(qs[n], ks[n], dn, preferred_element_type=F32) for n in hs]
        if has_alibi:
            off = ((j - i) * T).astype(F32)
            ss = [ss[n] + (al_ref[n] + sl_ref[n] * off) for n in hs]
        ms = [jnp.maximum(carry[n][0], jnp.max(ss[n], axis=-1, keepdims=True)) for n in hs]
        alphas = [jnp.exp2(carry[n][0] - ms[n]) for n in hs]
        ps = [jnp.exp2(ss[n] - ms[n]) for n in hs]
        ls = [alphas[n] * carry[n][1] + jnp.sum(ps[n], axis=-1, keepdims=True) for n in hs]
        vs = values(start)
        accs = [alphas[n] * carry[n][2] + jnp.dot(ps[n].astype(BF16), vs[n], preferred_element_type=F32)
                for n in hs]
        return tuple(zip(ms, ls, accs))

    def diagonal():
        start = pl.multiple_of(i * T, T)
        ks = keys(start)
        ss = [lax.dot_general(qs[n], ks[n], dn, preferred_element_type=F32) for n in hs]
        if has_alibi:
            ss = [ss[n] + al_ref[n] for n in hs]
        rows = lax.broadcasted_iota(jnp.int32, (T, T), 0)
        cols = lax.broadcasted_iota(jnp.int32, (T, T), 1)
        ss = [jnp.where(rows >= cols, s, NEG_BIG) for s in ss]
        ms = [jnp.max(s, axis=-1, keepdims=True) for s in ss]
        ps = [jnp.exp2(s - m) for s, m in zip(ss, ms)]
        ls = [jnp.sum(p, axis=-1, keepdims=True) for p in ps]
        vs = values(start)
        accs = [jnp.dot(ps[n].astype(BF16), vs[n], preferred_element_type=F32) for n in hs]
        return tuple(zip(ms, ls, accs))

    carry = diagonal()
    if has_alibi:
        reach = None
        for n in hs:
            q2 = qs[n][:, :LANES].astype(F32)
            q2 = jnp.sum(q2 * q2, axis=-1, keepdims=True)
            bound = (2.0 * jnp.sqrt(q2 * jnp.max(kmax_ref[n])) + EXP2_ZERO) / jnp.max(sl_ref[n])
            reach = bound if reach is None else jnp.maximum(reach, bound)
        reach = jnp.max(reach)
    else:
        reach = jnp.inf

    def cond(st):
        j = st[0]
        return jnp.logical_and(j >= 0, ((i - 1 - j) * T + 1).astype(F32) <= reach)

    _, carry = lax.while_loop(cond, lambda st: (st[0] - 1, tile(st[0], st[1])), (i - 1, carry))
    outs = []
    for n in hs:
        o = carry[n][2] / carry[n][1] * omul_ref[:, _head_lanes(n)].astype(F32)
        if og_row0 is not None:
            o = o * _lane_col(og_ref[...], og_row0 + pl.program_id(0) * NH + n)
        outs.append(o.astype(o_ref.dtype))
    o_ref[...] = _cat(outs)


def _slab_spec(rows, src, head_of, width=LANES):
    off = src[1] * LANES // width
    if rows is None:
        return pl.BlockSpec((src[0].shape[0], width), lambda h, i: (0, off + head_of(h)))
    return pl.BlockSpec((rows, width), lambda h, i: (i, off + head_of(h)))


def _same(h):
    return h


def _first(h):
    return 0


def _shared_spec(src, mode, NH, group):
    if mode == "head":
        return _slab_spec(None, src, _same, NH * LANES)
    if mode == "group":
        return _slab_spec(None, src, lambda hp: (hp * NH) // group)
    return _slab_spec(None, src, _first)


def _alibi_rows(n):
    sl = _alibi_slopes()[:, None, None] * LOG2E
    al = (sl * np.arange(n)[None, None, :]).astype(np.float32)
    return jnp.asarray(al), jnp.asarray(np.broadcast_to(sl, (N_HEADS, 1, n)).astype(np.float32))


def flash(q, k_parts, v, omul, *, dq, T, NH=FLASH_HEADS, group=1, qx=None, alibi=False,
          og=None, og_row0=0, name="flash"):
    S = q[0].shape[0]
    H = N_HEADS
    assert H % NH == 0 and (group == 1 or group % NH == 0)
    in_specs = [_slab_spec(T, q, _same, NH * dq)]
    args = [q[0]]
    if qx is not None:
        if qx[1] == "head":
            in_specs.append(pl.BlockSpec((NH, T, LANES), lambda hp, i: (hp, i, 0)))
        else:
            in_specs.append(pl.BlockSpec((None, T, LANES), lambda hp, i: ((hp * NH) // group, i, 0)))
        args.append(qx[0])
    for src, mode in k_parts:
        in_specs.append(_shared_spec(src, mode, NH, group))
        args.append(src[0])
    in_specs.append(_shared_spec(v[0], v[1], NH, group))
    args.append(v[0][0])
    if alibi:
        in_specs += [pl.BlockSpec((NH, 1, T), lambda hp, i: (hp, 0, 0))] * 2
        args += list(_alibi_rows(T))
    in_specs.append(_slab_spec(T, omul, _same, NH * LANES))
    args.append(omul[0])
    if og is not None:
        in_specs.append(pl.BlockSpec((T, LANES), lambda hp, i: (i, 0)))
        args.append(og)
    body = functools.partial(_flash_body, T, NH, dq, tuple(m for _, m in k_parts), v[1],
                             None if qx is None else qx[1], alibi, None if og is None else og_row0)
    return pl.pallas_call(
        body,
        out_shape=jax.ShapeDtypeStruct((S, H * LANES), BF16),
        grid=(H // NH, S // T),
        in_specs=in_specs,
        out_specs=pl.BlockSpec((T, NH * LANES), lambda hp, i: (i, hp)),
        scratch_shapes=[pltpu.VMEM((NH, 8, LANES), F32)] if alibi else [],
        compiler_params=_params("parallel", "arbitrary"),
        name=name,
    )(*args)


def _window_body(T, W, NH, og_row0, q_ref, k_ref, v_ref, al_ref, sl_ref, omul_ref, og_ref, o_ref):
    i = pl.program_id(1)
    hs = range(NH)
    span = W + T
    q0 = i * T
    start = pl.multiple_of(jnp.maximum(q0 - W, 0), T)
    k = k_ref[pl.ds(start, span), :]
    dn = (((1,), (1,)), ((), ()))
    ss = [lax.dot_general(q_ref[:, _head_lanes(n)], k, dn, preferred_element_type=F32) for n in hs]
    off = (start - q0).astype(F32)
    rows = lax.broadcasted_iota(jnp.int32, (T, span), 0)
    cols = lax.broadcasted_iota(jnp.int32, (T, span), 1)
    dist = rows - cols + (q0 - start)
    ok = (dist >= 0) & (dist < W)
    ss = [jnp.where(ok, ss[n] + (al_ref[n] + sl_ref[n] * off), NEG_BIG) for n in hs]
    ps = [jnp.exp2(s - jnp.max(s, axis=-1, keepdims=True)) for s in ss]
    ls = [jnp.sum(p, axis=-1, keepdims=True) for p in ps]
    v = v_ref[pl.ds(start, span), :]
    os_ = [jnp.dot(p.astype(BF16), v, preferred_element_type=F32) for p in ps]
    og = og_ref[...]
    outs = [(os_[n] / ls[n] * omul_ref[:, _head_lanes(n)].astype(F32)
             * _lane_col(og, og_row0 + pl.program_id(0) * NH + n)).astype(o_ref.dtype) for n in hs]
    o_ref[...] = _cat(outs)


def window_attention(q, k, v, omul, og, og_row0, *, group, T, NH=4, W=NSA_WINDOW):
    S = q[0].shape[0]
    span = W + T
    assert W % T == 0 and S >= span and group % NH == 0
    al, sl = _alibi_rows(span)
    vec = pl.BlockSpec((NH, 1, span), lambda hp, i: (hp, 0, 0))
    return pl.pallas_call(
        functools.partial(_window_body, T, W, NH, og_row0),
        out_shape=jax.ShapeDtypeStruct((S, N_HEADS * LANES), BF16),
        grid=(N_HEADS // NH, S // T),
        in_specs=[_slab_spec(T, q, _same, NH * LANES), _shared_spec(k, "group", NH, group),
                  _shared_spec(v, "group", NH, group), vec, vec, _slab_spec(T, omul, _same, NH * LANES),
                  pl.BlockSpec((T, LANES), lambda hp, i: (i, 0))],
        out_specs=pl.BlockSpec((T, NH * LANES), lambda hp, i: (i, hp)),
        compiler_params=_params("parallel", "arbitrary"),
        name="nsa_window",
    )(q[0], k[0], v[0], al, sl, omul[0], og)


def _block_onehot(S, block):
    e = (np.arange(S)[:, None] // block) == np.arange(LANES)[None, :]
    return jnp.asarray(e.astype(np.float32), dtype=BF16)


def _take_top(score, col, n):
    colf = col.astype(F32)
    left = score
    for _ in range(n):
        m = jnp.max(left, axis=-1, keepdims=True)
        idx = jnp.min(jnp.where(left == m, colf, float(LANES)), axis=-1, keepdims=True)
        left = jnp.where(colf == idx, -jnp.inf, left)
    return (left == -jnp.inf) & (score > -jnp.inf)


def _moba_select_body(T, NB, q_ref, k_ref, o_ref, km_ref):
    i = pl.program_id(1)

    @pl.when(i == 0)
    def _():
        km_ref[...] = jnp.zeros_like(km_ref)
        for b in range(NB):
            kb = k_ref[b * MOBA_BLOCK:(b + 1) * MOBA_BLOCK, :].astype(F32)
            km_ref[b:b + 1, :] = jnp.sum(kb, axis=0, keepdims=True) * (1.0 / MOBA_BLOCK)

    q = q_ref[...]
    km = km_ref[...]
    km_hi = km.astype(BF16)
    km_lo = (km - km_hi.astype(F32)).astype(BF16)
    dn = (((1,), (1,)), ((), ()))
    score = (lax.dot_general(q, km_hi, dn, preferred_element_type=F32)
             + lax.dot_general(q, km_lo, dn, preferred_element_type=F32))
    col = lax.broadcasted_iota(jnp.int32, (T, LANES), 1)
    row = lax.broadcasted_iota(jnp.int32, (T, LANES), 0)
    q_blk = (i * T + row) // MOBA_BLOCK
    score = jnp.where(col < q_blk, score, -jnp.inf)
    taken = _take_top(score, col, min(MOBA_TOPK, NB)) | (col == q_blk)
    o_ref[...] = jnp.where(taken, 0.0, NEG_BIG).astype(o_ref.dtype)


def moba_select(q, k, T=1024):
    S = q[0].shape[0]
    NB = S // MOBA_BLOCK
    return pl.pallas_call(
        functools.partial(_moba_select_body, T, NB),
        out_shape=jax.ShapeDtypeStruct((N_HEADS, S, LANES), BF16),
        grid=(N_HEADS, S // T),
        in_specs=[_slab_spec(T, q, _same), _slab_spec(None, k, _same)],
        out_specs=pl.BlockSpec((None, T, LANES), lambda h, i: (h, i, 0)),
        scratch_shapes=[pltpu.VMEM((LANES, LANES), F32)],
        compiler_params=_params("parallel", "arbitrary"),
        name="moba_select",
    )(q[0], k[0])


def _softplus(z):
    return jnp.maximum(z, 0.0) + jnp.log(1.0 + jnp.exp(-jnp.abs(z)))


def _stick_body(T, NH, q_ref, k_ref, v_ref, omul_ref, o_ref):
    i = pl.program_id(1)
    rows = lax.broadcasted_iota(jnp.int32, (T, T), 0)
    cols = lax.broadcasted_iota(jnp.int32, (T, T), 1)
    later = jnp.where(rows > cols, 1.0, 0.0).astype(BF16)
    before = cols < rows
    hs = range(NH)
    qs = [q_ref[:, _head_lanes(n)] for n in hs]
    dn = (((1,), (1,)), ((), ()))

    def tiles(j, state, diag):
        start = pl.multiple_of(j * T, T)
        zs = [lax.dot_general(qs[n], k_ref[pl.ds(start, T), _head_lanes(n)], dn,
                              preferred_element_type=F32) for n in hs]
        log_keeps = [-_softplus(z) for z in zs]
        if diag:
            log_keeps = [jnp.where(before, lk, 0.0) for lk in log_keeps]
        his = [lk.astype(BF16) for lk in log_keeps]
        los = [(lk - hi.astype(F32)).astype(BF16) for lk, hi in zip(log_keeps, his)]
        b_his = [jnp.dot(hi, later, preferred_element_type=F32) for hi in his]
        b_los = [jnp.dot(lo, later, preferred_element_type=F32) for lo in los]
        aa = [jnp.exp(zs[n] + log_keeps[n] + (b_his[n] + b_los[n]) + state[n][0]) for n in hs]
        if diag:
            aa = [jnp.where(before, a, 0.0) for a in aa]
        accs = [state[n][1] + jnp.dot(aa[n].astype(BF16), v_ref[pl.ds(start, T), _head_lanes(n)],
                                      preferred_element_type=F32) for n in hs]
        runs = [state[n][0] + jnp.sum(log_keeps[n], axis=-1, keepdims=True) for n in hs]
        return tuple(zip(runs, accs))

    state = tiles(i, tuple((jnp.zeros((T, 1), F32), jnp.zeros((T, LANES), F32)) for _ in hs), True)

    def cond(st):
        j, state = st
        top = state[0][0]
        for run, _ in state[1:]:
            top = jnp.maximum(top, run)
        return jnp.logical_and(j >= 0, jnp.max(top) > SB_LOG_CUTOFF)

    def step(st):
        j, state = st
        return j - 1, tiles(j, state, False)

    _, state = lax.while_loop(cond, step, (i - 1, state))
    o = _cat([acc for _, acc in state])
    o_ref[...] = (o * omul_ref[...].astype(F32)).astype(o_ref.dtype)


def stick_attention(q, k, v, omul, T, NH=STICK_HEADS):
    S = q[0].shape[0]
    width = NH * LANES
    grid = (N_HEADS // NH, S // T)
    return pl.pallas_call(
        functools.partial(_stick_body, T, NH),
        out_shape=jax.ShapeDtypeStruct((S, MIX_WIDTH), BF16),
        grid=grid,
        in_specs=[_slab_spec(T, q, _same, width), _slab_spec(None, k, _same, width),
                  _slab_spec(None, v, _same, width), _slab_spec(T, omul, _same, width)],
        out_specs=pl.BlockSpec((T, width), lambda h, i: (i, h)),
        compiler_params=_params("parallel", "arbitrary"),
        name="stick_attention",
    )(q[0], k[0], v[0], omul[0])


def _nsa_compress_body(norm, x_ref, w_ref, pos_ref, g_ref, o_ref):
    half = NSA_CMP_STRIDE * HEAD_DIM
    x = x_ref[...]
    first = jnp.dot(x, w_ref[:half, :], preferred_element_type=F32)
    second = jnp.dot(x, w_ref[half:, :], preferred_element_type=F32)
    bias = jnp.dot(pos_ref[...], w_ref[...], preferred_element_type=F32)
    n = x.shape[0]
    y = first + pltpu.roll(second, n - 1, axis=0) + bias[0:1, :]
    if norm:
        y = _head_rms(y, g_ref[...], HEAD_DIM)
    o_ref[...] = y.astype(o_ref.dtype)


def nsa_compress(x, w, pos, gain, norm):
    G, n, width = x.shape
    pos_flat = jnp.broadcast_to(pos.reshape(1, -1), (16, pos.size)).astype(BF16)
    return pl.pallas_call(
        functools.partial(_nsa_compress_body, norm),
        out_shape=jax.ShapeDtypeStruct((G, n, HEAD_DIM), BF16),
        grid=(G,),
        in_specs=[pl.BlockSpec((None, n, width), lambda g: (g, 0, 0)),
                  pl.BlockSpec(w.shape, lambda g: (0, 0)),
                  pl.BlockSpec((16, pos.size), lambda g: (0, 0)),
                  pl.BlockSpec((1, HEAD_DIM), lambda g: (0, 0))],
        out_specs=pl.BlockSpec((None, n, HEAD_DIM), lambda g: (g, 0, 0)),
        compiler_params=_params("parallel"),
        name="nsa_compress",
    )(x, w.astype(BF16), pos_flat, gain.reshape(1, HEAD_DIM))


def _nsa_cmp_body(T, n_cmp, n_sel, q_ref, kc_ref, vc_ref, c2s_ref, sl_ref, omul_ref, og_ref,
                  o_ref, sel_ref):
    g, i = pl.program_id(0), pl.program_id(1)
    NC = kc_ref.shape[0]
    kc = kc_ref[...]
    c2s = c2s_ref[...]
    vc_c2s = jnp.concatenate([vc_ref[...], c2s], axis=-1)
    og = og_ref[...]
    row = lax.broadcasted_iota(jnp.int32, (T, NC), 0)
    col = lax.broadcasted_iota(jnp.int32, (T, NC), 1)
    dist = i * T + row - (col * NSA_CMP_STRIDE + (NSA_CMP_BLOCK - 1))
    ok = (dist >= 0) & (col < n_cmp)
    dist_f = dist.astype(F32)
    imp = jnp.zeros((T, LANES), F32)
    outs = []
    for p in range(NSA_HPG):
        qp = q_ref[:, p * LANES:(p + 1) * LANES]
        s = lax.dot_general(qp, kc, (((1,), (1,)), ((), ())), preferred_element_type=F32)
        s = jnp.where(ok, s - sl_ref[p] * dist_f, NEG_BIG)
        m = jnp.max(s, axis=-1, keepdims=True)
        e = jnp.exp2(s - m)
        l = jnp.sum(e, axis=-1, keepdims=True)
        inv = jnp.where(m > 0.5 * NEG_BIG, 1.0 / l, 0.0)
        prob = e * inv
        hi = prob.astype(BF16)
        lo = (prob - hi.astype(F32)).astype(BF16)
        both = jnp.dot(hi, vc_c2s, preferred_element_type=F32)
        o = both[:, :LANES]
        imp = imp + (both[:, LANES:] + jnp.dot(lo, c2s, preferred_element_type=F32))
        o = o * _lane_col(og, g * NSA_HPG + p) * omul_ref[:, p * LANES:(p + 1) * LANES].astype(F32)
        outs.append(o.astype(o_ref.dtype))
    o_ref[...] = jnp.concatenate(outs, axis=-1)

    blk = lax.broadcasted_iota(jnp.int32, (T, LANES), 1)
    cur = (i * T + lax.broadcasted_iota(jnp.int32, (T, LANES), 0)) // NSA_SEL_BLOCK
    kn = min(NSA_SEL_N, n_sel)
    forced = ((blk == 0) | (blk == cur) | (blk == cur - 1)) & (blk <= cur)
    imp = jnp.where((blk > cur) | (blk >= n_sel) | forced, -jnp.inf, imp)
    taken = forced | _take_top(imp, blk, kn - 3)
    sel_ref[...] = jnp.where(taken, 0.0, NEG_BIG).astype(sel_ref.dtype)


def nsa_compressed(q, k_cmp, v_cmp, omul, og, T):
    S = q[0].shape[0]
    G, NC, _ = k_cmp.shape
    n_cmp = (S - NSA_CMP_BLOCK) // NSA_CMP_STRIDE + 1
    n_sel = S // NSA_SEL_BLOCK
    assert 3 <= n_sel <= LANES and NC >= n_cmp
    c_start = np.arange(NC)[:, None] * NSA_CMP_STRIDE
    s_start = np.arange(LANES)[None, :] * NSA_SEL_BLOCK
    c2s = ((c_start < s_start + NSA_SEL_BLOCK) & (c_start + NSA_CMP_BLOCK > s_start)
           & (np.arange(NC)[:, None] < n_cmp) & (np.arange(LANES)[None, :] < n_sel))
    slopes = np.broadcast_to(_alibi_slopes()[:, None, None] * LOG2E, (N_HEADS, 1, NC)).astype(np.float32)
    gw = NSA_HPG * LANES
    cmp_spec = pl.BlockSpec((None, NC, HEAD_DIM), lambda g, i: (g, 0, 0))
    return pl.pallas_call(
        functools.partial(_nsa_cmp_body, T, n_cmp, n_sel),
        out_shape=[jax.ShapeDtypeStruct((S, MIX_WIDTH), BF16),
                   jax.ShapeDtypeStruct((G, S, LANES), BF16)],
        grid=(G, S // T),
        in_specs=[_slab_spec(T, q, _same, gw), cmp_spec, cmp_spec,
                  pl.BlockSpec((NC, LANES), lambda g, i: (0, 0)),
                  pl.BlockSpec((NSA_HPG, 1, NC), lambda g, i: (g, 0, 0)),
                  _slab_spec(T, omul, _same, gw),
                  pl.BlockSpec((T, LANES), lambda g, i: (i, 0))],
        out_specs=[pl.BlockSpec((T, gw), lambda g, i: (i, g)),
                   pl.BlockSpec((None, T, LANES), lambda g, i: (g, i, 0))],
        compiler_params=_params("parallel", "parallel"),
        name="nsa_compressed",
    )(q[0], k_cmp, v_cmp, jnp.asarray(c2s.astype(np.float32), dtype=BF16), jnp.asarray(slopes),
      omul[0], og)


def _row(v):
    return v.reshape(1, -1).astype(F32)


def _rope_tables(S):
    inv_freq = ROPE_THETA ** (-jnp.arange(0, MLA_ROPE, 2, dtype=F32) / MLA_ROPE)
    ang = jnp.arange(S, dtype=F32)[:, None] * inv_freq[None, :]
    cos, sin = jnp.cos(ang), jnp.sin(ang)
    z = jnp.zeros_like(cos)
    return (jnp.concatenate([cos, z, cos, z], axis=-1),
            jnp.concatenate([-sin, z, sin, z], axis=-1))


def _rope_lanes(v):
    half = MLA_ROPE // 2
    z = jnp.zeros(v.shape[:-1] + (half,), v.dtype)
    return jnp.concatenate([v[..., :half], z, v[..., half:], z], axis=-1)


def mla_mixer(h, w_in, q_lat_norm, w_q_up, kv_lat_norm, w_kv_up, q_norm, k_norm):
    S = h.shape[0]
    c1 = MLA_Q_RANK
    c2 = c1 + MLA_KV_RANK
    c3 = c2 + MLA_ROPE
    scale = MLA_QK_DIM ** -0.5 * LOG2E
    cos, sin = _rope_tables(S)
    tables = [("cos", cos, "row"), ("sin", sin, "row")]

    w_lat = jnp.concatenate([w_in[:, :c2], _rope_lanes(w_in[:, c2:c3]),
                             jnp.zeros((w_in.shape[0], PROJ_TN - c2 - LANES), w_in.dtype)], axis=-1)
    lat = proj(h, w_lat, [(0, 1, _epi_mla_in)],
               [("g_qlat", _row(q_lat_norm), "vec"), ("g_kvlat", _row(kv_lat_norm), "vec"),
                ("gk_r", _row(_rope_lanes(k_norm[MLA_NOPE:])), "vec")] + tables, name="mla_latents")
    gate = proj(h, w_in[:, c3:], [(0, 2, _epi_silu)], name="mla_gate")

    wq = w_q_up.reshape(MLA_Q_RANK, N_HEADS, MLA_QK_DIM)
    wq = jnp.concatenate([wq[..., :MLA_NOPE], _rope_lanes(wq[..., MLA_NOPE:])], axis=-1)
    q = proj(lat, wq.reshape(MLA_Q_RANK, N_HEADS * 2 * LANES).astype(BF16), [(0, 4, _epi_mla_q(scale))],
             [("gq_n", _row(q_norm[:MLA_NOPE]), "vec"), ("gq_r", _row(_rope_lanes(q_norm[MLA_NOPE:])), "vec")]
             + tables, name="mla_q")
    wkv = w_kv_up.reshape(MLA_KV_RANK, N_HEADS, MLA_NOPE + HEAD_DIM)
    wkv = jnp.concatenate([wkv[..., :MLA_NOPE].reshape(MLA_KV_RANK, -1),
                           wkv[..., MLA_NOPE:].reshape(MLA_KV_RANK, -1)], axis=-1).astype(BF16)
    kv = proj(lat, wkv, [(0, 2, _epi_headnorm("gk_n")), (2, 4, _epi_scale(1.0))],
              [("gk_n", _row(k_norm[:MLA_NOPE]), "vec")], h_block=c1 // MLA_KV_RANK, name="mla_kv")
    k_rope = (lat, c2 // LANES)
    return [flash((q, 0), [((kv, 0), "head"), (k_rope, "all")], ((kv, N_HEADS), "head"), (gate, 0),
                  dq=2 * LANES, T=min(FLASH_T, S), name="mla_attention")]


def moba_mixer(h, w_in, q_norm, k_norm):
    S = h.shape[0]
    scale = HEAD_DIM ** -0.5 * LOG2E
    n = MIX_WIDTH // PROJ_TN
    qkvg = proj(h, w_in, [(0, n, _epi_headnorm("gq", scale)), (n, 2 * n, _epi_headnorm("gk")),
                          (2 * n, 3 * n, _epi_scale(1.0)), (3 * n, 4 * n, _epi_silu)],
                [("gq", _row(q_norm), "vec"), ("gk", _row(k_norm), "vec")], name="moba_in")
    H = N_HEADS
    q, k, v, gate = (qkvg, 0), (qkvg, H), (qkvg, 2 * H), (qkvg, 3 * H)
    bias = moba_select(q, k)
    return [flash(q, [(k, "head"), ((_block_onehot(S, MOBA_BLOCK), 0), "all")], (v, "head"), gate, dq=LANES,
                  qx=(bias, "head"), alibi=True, T=min(FLASH_T, S), name="moba_attention")]


def stick_mixer(h, w_in):
    S = h.shape[0]
    scale = HEAD_DIM ** -0.5
    n = MIX_WIDTH // PROJ_TN
    qkvg = proj(h, w_in, [(0, n, _epi_scale(scale)), (n, 3 * n, _epi_scale(1.0)), (3 * n, 4 * n, _epi_silu)],
                name="stick_in")
    H = N_HEADS
    return [stick_attention((qkvg, 0), (qkvg, H), (qkvg, 2 * H), (qkvg, 3 * H), T=min(STICK_T, S))]


def nsa_mixer(h, w_in, q_norm, k_norm, w_cmp_k, w_cmp_v, cmp_pos):
    S = h.shape[0]
    G, P, H = NSA_GROUPS, NSA_HPG, N_HEADS
    W, KV = MIX_WIDTH, NSA_KV_WIDTH
    half = KV // 2
    scale = HEAD_DIM ** -0.5 * LOG2E
    cuts = np.cumsum([W, KV, KV, KV, 3 * H]).tolist()
    assert KV == PROJ_TN and W % PROJ_TN == 0
    n = W // PROJ_TN
    main = proj(h, w_in, [(0, n, _epi_headnorm("gq", scale)), (n, n + 1, _epi_scale(1.0)),
                          (n + 1, n + 2, _epi_headnorm("gk1", n_norm=half)),
                          (n + 2, n + 3, _epi_headnorm("gk2", n_norm=half))],
                [("gq", _row(q_norm), "vec"), ("gk1", _row(k_norm[1]), "vec"), ("gk2", _row(k_norm[2]), "vec")],
                n_tiles=n + 3, name="nsa_in")
    q = (main, 0)
    kc0 = H
    ks, vs = (main, kc0 + 2 * G), (main, kc0 + 3 * G)
    kw, vw = (main, kc0 + 4 * G), (main, kc0 + 5 * G)
    wg = w_in[:, cuts[3]:cuts[4]].reshape(-1, H, 3).transpose(0, 2, 1).reshape(-1, 3 * H)
    wg = jnp.pad(wg, ((0, 0), (0, LANES - 3 * H)))
    og = proj(h, wg.astype(BF16), [(0, 1, _epi_sigmoid)], tn=LANES, out_dtype=F32, name="nsa_branch_gates")
    gate = (proj(h, w_in[:, cuts[4]:], [(0, n, _epi_silu)], name="nsa_gate"), 0)

    def chunks(t):
        return t.reshape(S // NSA_CMP_STRIDE, NSA_CMP_STRIDE, G, HEAD_DIM).transpose(2, 0, 1, 3).reshape(
            G, S // NSA_CMP_STRIDE, NSA_CMP_STRIDE * HEAD_DIM)

    c0 = kc0 * LANES
    k_cmp = nsa_compress(chunks(main[:, c0:c0 + half]), w_cmp_k, cmp_pos, k_norm[0], True)
    v_cmp = nsa_compress(chunks(main[:, c0 + half:c0 + KV]), w_cmp_v, cmp_pos, k_norm[0], False)
    o_cmp, bias = nsa_compressed(q, k_cmp, v_cmp, gate, og, T=min(NSA_CMP_T, S))
    o_slc = flash(q, [(ks, "group"), ((_block_onehot(S, NSA_SEL_BLOCK), 0), "all")], (vs, "group"), gate,
                  dq=LANES, group=P, qx=(bias, "group"), alibi=True, og=og, og_row0=H,
                  T=min(FLASH_T, S), name="nsa_selected")
    o_win = window_attention(q, kw, vw, gate, og, 2 * H, group=P, T=min(WINDOW_T, S))
    return [o_cmp, o_slc, o_win]


def kernel(x, norm_a, w_in_a, q_lat_norm_a, w_q_up_a, kv_lat_norm_a, w_kv_up_a, q_norm_a, k_norm_a, w_out_a,
           norm_b, w_in_b, q_norm_b, k_norm_b, w_out_b,
           norm_c, w_in_c, w_out_c,
           norm_d, w_in_d, q_norm_d, k_norm_d, w_cmp_k_d, w_cmp_v_d, cmp_pos_d, w_out_d):
    B, S, D = x.shape
    norms = (norm_a, norm_b, norm_c, norm_d)
    w_outs = (w_out_a, w_out_b, w_out_c, w_out_d)
    depth = norm_a.shape[0] + norm_b.shape[0] + norm_c.shape[0] + norm_d.shape[0]
    rows = x.reshape(B * S, D)
    outs = []
    for b in range(B):
        xb = rows if B == 1 else rows[b * S:(b + 1) * S]
        h = rmsnorm_rows(xb, norms[0][0])
        for layer in range(depth):
            m, j = layer % 4, layer // 4
            if m == 0:
                a = mla_mixer(h, w_in_a[j], q_lat_norm_a[j], w_q_up_a[j], kv_lat_norm_a[j], w_kv_up_a[j],
                              q_norm_a[j], k_norm_a[j])
            elif m == 1:
                a = moba_mixer(h, w_in_b[j], q_norm_b[j], k_norm_b[j])
            elif m == 2:
                a = stick_mixer(h, w_in_c[j])
            else:
                a = nsa_mixer(h, w_in_d[j], q_norm_d[j], k_norm_d[j], w_cmp_k_d[j], w_cmp_v_d[j], cmp_pos_d[j])
            nxt = layer + 1
            g_next = norms[nxt % 4][nxt // 4] if nxt < depth else None
            xb, h = outproj(a, w_outs[m][j].astype(BF16), xb, g_next)
        outs.append(xb)
    out = outs[0] if B == 1 else jnp.concatenate(outs, axis=0)
    return out.reshape(B, S, D)
```

```python
import functools

import numpy as np
import jax
import jax.numpy as jnp
from jax import lax
from jax.experimental import pallas as pl
from jax.experimental.pallas import tpu as pltpu

N_HEADS = 16
HEAD_DIM = 128
MIX_WIDTH = N_HEADS * HEAD_DIM
EPS = 1e-6
NEG_BIG = -1e30
LANES = 128

MLA_Q_RANK = 512
MLA_KV_RANK = 256
MLA_NOPE = 128
MLA_ROPE = 64
MLA_QK_DIM = MLA_NOPE + MLA_ROPE
ROPE_THETA = 10000.0

MOBA_BLOCK = 256
MOBA_TOPK = 3

NSA_GROUPS = 4
NSA_HPG = N_HEADS // NSA_GROUPS
NSA_CMP_BLOCK = 32
NSA_CMP_STRIDE = 16
NSA_SEL_BLOCK = 64
NSA_SEL_N = 16
NSA_WINDOW = 512
NSA_KV_WIDTH = 2 * NSA_GROUPS * HEAD_DIM

SB_LOG_CUTOFF = -110.0
EXP2_ZERO = 151.0

VMEM_LIMIT = 56 * 1024 * 1024

LOG2E = 1.4426950408889634

FLASH_T = 1024
WINDOW_T = 512
STICK_T = 256
NSA_CMP_T = 512
FLASH_HEADS = 2
STICK_HEADS = 4
PROJ_TM = 1024
PROJ_TN = 1024
OUTPROJ_TM = 512

BF16 = jnp.bfloat16
F32 = jnp.float32


def _params(*sem):
    return pltpu.CompilerParams(dimension_semantics=sem, vmem_limit_bytes=VMEM_LIMIT)


def _alibi_slopes():
    return 2.0 ** (-8.0 * np.arange(1, N_HEADS + 1) / N_HEADS)


def _rmsnorm_body(x_ref, g_ref, o_ref):
    x = x_ref[...]
    ms = jnp.mean(x * x, axis=-1, keepdims=True)
    o_ref[...] = (x * lax.rsqrt(ms + EPS) * g_ref[...]).astype(o_ref.dtype)


def rmsnorm_rows(x, g, tm=512):
    S, D = x.shape
    return pl.pallas_call(
        _rmsnorm_body,
        out_shape=jax.ShapeDtypeStruct((S, D), BF16),
        grid=(S // tm,),
        in_specs=[pl.BlockSpec((tm, D), lambda i: (i, 0)),
                  pl.BlockSpec((1, D), lambda i: (0, 0))],
        out_specs=pl.BlockSpec((tm, D), lambda i: (i, 0)),
        compiler_params=_params("parallel"),
        name="rmsnorm_rows",
    )(x, g.reshape(1, D))


def _head_rms(y, gain, n_real):
    ms = jnp.sum(y * y, axis=-1, keepdims=True) * (1.0 / n_real)
    return y * lax.rsqrt(ms + EPS) * gain


def _cat(parts):
    return parts[0] if len(parts) == 1 else jnp.concatenate(parts, axis=-1)


def _epi_scale(scale):
    def epi(acc, ex):
        return acc * scale if scale != 1.0 else acc
    return epi


def _epi_silu(acc, ex):
    return acc / (1.0 + jnp.exp(-acc))


def _epi_sigmoid(acc, ex):
    return 1.0 / (1.0 + jnp.exp(-acc))


def _epi_headnorm(gain, scale=1.0, n_norm=None):
    def epi(acc, ex):
        n = acc.shape[-1] if n_norm is None else n_norm
        g = ex[gain][...] * scale
        parts = [_head_rms(acc[:, c:c + LANES], g, LANES) for c in range(0, n, LANES)]
        if n < acc.shape[-1]:
            parts.append(acc[:, n:])
        return _cat(parts)
    return epi


def _rope_slab(y, gain, cos, sin, scale):
    y = _head_rms(y, gain, MLA_ROPE)
    return (y * cos + pltpu.roll(y, 64, axis=1) * sin) * scale


def _epi_mla_q(scale):
    def epi(acc, ex):
        gn = ex["gq_n"][...] * scale
        gr, cos, sin = ex["gq_r"][...], ex["cos"][...], ex["sin"][...]
        parts = []
        for c in range(0, acc.shape[-1], 2 * LANES):
            parts.append(_head_rms(acc[:, c:c + LANES], gn, MLA_NOPE))
            parts.append(_rope_slab(acc[:, c + LANES:c + 2 * LANES], gr, cos, sin, scale))
        return _cat(parts)
    return epi


def _epi_mla_in(acc, ex):
    c1, c2 = MLA_Q_RANK, MLA_Q_RANK + MLA_KV_RANK
    q_lat = _head_rms(acc[:, :c1], ex["g_qlat"][...], c1)
    kv_lat = _head_rms(acc[:, c1:c2], ex["g_kvlat"][...], MLA_KV_RANK)
    k_rope = _rope_slab(acc[:, c2:c2 + LANES], ex["gk_r"][...], ex["cos"][...], ex["sin"][...], 1.0)
    return _cat([q_lat, kv_lat, k_rope, jnp.zeros_like(k_rope)])


def _proj_body(epis, names, cast, h_ref, w_ref, *refs):
    n = len(names)
    ex = dict(zip(names, refs[:n]))
    o_ref = refs[n]
    j, i = pl.program_id(0), pl.program_id(1)
    if cast:
        wb_ref = refs[n + 1]

        @pl.when(i == 0)
        def _():
            wb_ref[...] = w_ref[...].astype(BF16)

        w = wb_ref[...]
    else:
        w = w_ref[...]
    acc = jnp.dot(h_ref[...], w, preferred_element_type=F32)
    if len(epis) == 1:
        o_ref[...] = epis[0][2](acc, ex).astype(o_ref.dtype)
    else:
        for t0, t1, epi in epis:
            @pl.when((j >= t0) & (j < t1))
            def _(epi=epi):
                o_ref[...] = epi(acc, ex).astype(o_ref.dtype)


def proj(h, w, epis, extra=(), *, h_block=0, w_tile0=0, n_tiles=None, tn=PROJ_TN, out_dtype=BF16, name="proj"):
    S = h.shape[0]
    K = w.shape[0]
    tm = PROJ_TM
    n_tiles = (w.shape[1] // tn - w_tile0) if n_tiles is None else n_tiles
    cast = w.dtype != BF16
    in_specs = [pl.BlockSpec((tm, K), lambda j, i: (i, h_block)),
                pl.BlockSpec((K, tn), lambda j, i: (0, w_tile0 + j))]
    args = [h, w]
    names = []
    for nm, arr, kind in extra:
        names.append(nm)
        if kind == "vec":
            in_specs.append(pl.BlockSpec(arr.shape, lambda j, i: (0, 0)))
        else:
            in_specs.append(pl.BlockSpec((tm, LANES), lambda j, i: (i, 0)))
        args.append(arr)
    return pl.pallas_call(
        functools.partial(_proj_body, tuple(epis), tuple(names), cast),
        out_shape=jax.ShapeDtypeStruct((S, n_tiles * tn), out_dtype),
        grid=(n_tiles, S // tm),
        in_specs=in_specs,
        out_specs=pl.BlockSpec((tm, tn), lambda j, i: (i, j)),
        scratch_shapes=[pltpu.VMEM((K, tn), BF16)] if cast else [],
        compiler_params=_params("arbitrary", "arbitrary"),
        name=name,
    )(*args)


def _outproj_body(n_a, has_next, *refs):
    a_refs = refs[:n_a]
    w_ref, x_ref = refs[n_a], refs[n_a + 1]
    if has_next:
        g_ref, xo_ref, ho_ref = refs[n_a + 2:n_a + 5]
    else:
        xo_ref = refs[n_a + 2]
    if n_a == 1:
        a = a_refs[0][...]
    else:
        a = a_refs[0][...].astype(F32)
        for r in a_refs[1:]:
            a = a + r[...].astype(F32)
        a = a.astype(BF16)
    xn = x_ref[...] + jnp.dot(a, w_ref[...], preferred_element_type=F32)
    xo_ref[...] = xn
    if has_next:
        ms = jnp.mean(xn * xn, axis=-1, keepdims=True)
        ho_ref[...] = (xn * lax.rsqrt(ms + EPS) * g_ref[...]).astype(BF16)


def outproj(a_list, w, x, g_next=None):
    S, D = x.shape
    n_a = len(a_list)
    tm = OUTPROJ_TM // 2 if n_a > 1 else OUTPROJ_TM
    has_next = g_next is not None
    row = pl.BlockSpec((tm, D), lambda i: (i, 0))
    in_specs = [pl.BlockSpec((tm, MIX_WIDTH), lambda i: (i, 0)) for _ in a_list]
    in_specs += [pl.BlockSpec((MIX_WIDTH, D), lambda i: (0, 0)), row]
    args = list(a_list) + [w, x]
    out_shape = [jax.ShapeDtypeStruct((S, D), F32)]
    out_specs = [row]
    if has_next:
        in_specs.append(pl.BlockSpec((1, D), lambda i: (0, 0)))
        args.append(g_next.reshape(1, D))
        out_shape.append(jax.ShapeDtypeStruct((S, D), BF16))
        out_specs.append(row)
    res = pl.pallas_call(
        functools.partial(_outproj_body, n_a, has_next),
        out_shape=out_shape,
        grid=(S // tm,),
        in_specs=in_specs,
        out_specs=out_specs,
        compiler_params=_params("parallel"),
        name="outproj",
    )(*args)
    return (res[0], res[1]) if has_next else (res[0], None)


def _lane_col(x, c):
    lane = lax.broadcasted_iota(jnp.int32, x.shape, 1)
    return jnp.sum(jnp.where(lane == c, x, 0.0), axis=-1, keepdims=True)


def _head_lanes(n):
    return slice(n * LANES, (n + 1) * LANES)


def _flash_body(T, NH, dq, k_modes, v_mode, qx_mode, has_alibi, og_row0, *refs):
    pos = 0
    q_ref = refs[pos]; pos += 1
    if qx_mode is not None:
        qx_ref = refs[pos]; pos += 1
    k_refs = refs[pos:pos + len(k_modes)]; pos += len(k_modes)
    v_ref = refs[pos]; pos += 1
    if has_alibi:
        al_ref, sl_ref = refs[pos], refs[pos + 1]; pos += 2
    omul_ref = refs[pos]; pos += 1
    if og_row0 is not None:
        og_ref = refs[pos]; pos += 1
    o_ref = refs[pos]

    i = pl.program_id(1)
    hs = range(NH)
    if has_alibi:
        kmax_ref = refs[pos + 1]

        @pl.when(i == 0)
        def _():
            for n in hs:
                kn = k_refs[0][:, _head_lanes(n)] if k_modes[0] == "head" else k_refs[0][...]
                kn = kn.astype(F32)
                k2 = jnp.sum(kn * kn, axis=-1, keepdims=True)
                kmax_ref[n] = jnp.broadcast_to(jnp.max(k2, axis=0, keepdims=True), kmax_ref.shape[1:])

    qs = []
    for n in hs:
        qn = q_ref[:, n * dq:(n + 1) * dq]
        if qx_mode == "head":
            qn = jnp.concatenate([qn, qx_ref[n]], axis=-1)
        elif qx_mode == "group":
            qn = jnp.concatenate([qn, qx_ref[...]], axis=-1)
        qs.append(qn)
    dn = (((1,), (1,)), ((), ()))

    def keys(start):
        parts = [kr[pl.ds(start, T), :] for kr in k_refs]
        return [_cat([p[:, _head_lanes(n)] if mode == "head" else p for p, mode in zip(parts, k_modes)])
                for n in hs]

    def values(start):
        v = v_ref[pl.ds(start, T), :]
        return [v[:, _head_lanes(n)] if v_mode == "head" else v for n in hs]

    def tile(j, carry):
        start = pl.multiple_of(j * T, T)
        ks = keys(start)
        ss = [lax.dot_general(qs[n], ks[n], dn, preferred_element_type=F32) for n in hs]
        if has_alibi:
            off = ((j - i) * T).astype(F32)
            ss = [ss[n] + (al_ref[n] + sl_ref[n] * off) for n in hs]
        ms = [jnp.maximum(carry[n][0], jnp.max(ss[n], axis=-1, keepdims=True)) for n in hs]
        alphas = [jnp.exp2(carry[n][0] - ms[n]) for n in hs]
        ps = [jnp.exp2(ss[n] - ms[n]) for n in hs]
        ls = [alphas[n] * carry[n][1] + jnp.sum(ps[n], axis=-1, keepdims=True) for n in hs]
        vs = values(start)
        accs = [alphas[n] * carry[n][2] + jnp.dot(ps[n].astype(BF16), vs[n], preferred_element_type=F32)
                for n in hs]
        return tuple(zip(ms, ls, accs))

    def diagonal():
        start = pl.multiple_of(i * T, T)
        ks = keys(start)
        ss = [lax.dot_general(qs[n], ks[n], dn, preferred_element_type=F32) for n in hs]
        if has_alibi:
            ss = [ss[n] + al_ref[n] for n in hs]
        rows = lax.broadcasted_iota(jnp.int32, (T, T), 0)
        cols = lax.broadcasted_iota(jnp.int32, (T, T), 1)
        ss = [jnp.where(rows >= cols, s, NEG_BIG) for s in ss]
        ms = [jnp.max(s, axis=-1, keepdims=True) for s in ss]
        ps = [jnp.exp2(s - m) for s, m in zip(ss, ms)]
        ls = [jnp.sum(p, axis=-1, keepdims=True) for p in ps]
        vs = values(start)
        accs = [jnp.dot(ps[n].astype(BF16), vs[n], preferred_element_type=F32) for n in hs]
        return tuple(zip(ms, ls, accs))

    carry = diagonal()
    if has_alibi:
        reach = None
        for n in hs:
            q2 = qs[n][:, :LANES].astype(F32)
            q2 = jnp.sum(q2 * q2, axis=-1, keepdims=True)
            bound = (2.0 * jnp.sqrt(q2 * jnp.max(kmax_ref[n])) + EXP2_ZERO) / jnp.max(sl_ref[n])
            reach = bound if reach is None else jnp.maximum(reach, bound)
        reach = jnp.max(reach)
    else:
        reach = jnp.inf

    def cond(st):
        j = st[0]
        return jnp.logical_and(j >= 0, ((i - 1 - j) * T + 1).astype(F32) <= reach)

    _, carry = lax.while_loop(cond, lambda st: (st[0] - 1, tile(st[0], st[1])), (i - 1, carry))
    outs = []
    for n in hs:
        o = carry[n][2] / carry[n][1] * omul_ref[:, _head_lanes(n)].astype(F32)
        if og_row0 is not None:
            o = o * _lane_col(og_ref[...], og_row0 + pl.program_id(0) * NH + n)
        outs.append(o.astype(o_ref.dtype))
    o_ref[...] = _cat(outs)


def _slab_spec(rows, src, head_of, width=LANES):
    off = src[1] * LANES // width
    if rows is None:
        return pl.BlockSpec((src[0].shape[0], width), lambda h, i: (0, off + head_of(h)))
    return pl.BlockSpec((rows, width), lambda h, i: (i, off + head_of(h)))


def _same(h):
    return h


def _first(h):
    return 0


def _shared_spec(src, mode, NH, group):
    if mode == "head":
        return _slab_spec(None, src, _same, NH * LANES)
    if mode == "group":
        return _slab_spec(None, src, lambda hp: (hp * NH) // group)
    return _slab_spec(None, src, _first)


def _alibi_rows(n):
    sl = _alibi_slopes()[:, None, None] * LOG2E
    al = (sl * np.arange(n)[None, None, :]).astype(np.float32)
    return jnp.asarray(al), jnp.asarray(np.broadcast_to(sl, (N_HEADS, 1, n)).astype(np.float32))


def flash(q, k_parts, v, omul, *, dq, T, NH=FLASH_HEADS, group=1, qx=None, alibi=False,
          og=None, og_row0=0, name="flash"):
    S = q[0].shape[0]
    H = N_HEADS
    assert H % NH == 0 and (group == 1 or group % NH == 0)
    in_specs = [_slab_spec(T, q, _same, NH * dq)]
    args = [q[0]]
    if qx is not None:
        if qx[1] == "head":
            in_specs.append(pl.BlockSpec((NH, T, LANES), lambda hp, i: (hp, i, 0)))
        else:
            in_specs.append(pl.BlockSpec((None, T, LANES), lambda hp, i: ((hp * NH) // group, i, 0)))
        args.append(qx[0])
    for src, mode in k_parts:
        in_specs.append(_shared_spec(src, mode, NH, group))
        args.append(src[0])
    in_specs.append(_shared_spec(v[0], v[1], NH, group))
    args.append(v[0][0])
    if alibi:
        in_specs += [pl.BlockSpec((NH, 1, T), lambda hp, i: (hp, 0, 0))] * 2
        args += list(_alibi_rows(T))
    in_specs.append(_slab_spec(T, omul, _same, NH * LANES))
    args.append(omul[0])
    if og is not None:
        in_specs.append(pl.BlockSpec((T, LANES), lambda hp, i: (i, 0)))
        args.append(og)
    body = functools.partial(_flash_body, T, NH, dq, tuple(m for _, m in k_parts), v[1],
                             None if qx is None else qx[1], alibi, None if og is None else og_row0)
    return pl.pallas_call(
        body,
        out_shape=jax.ShapeDtypeStruct((S, H * LANES), BF16),
        grid=(H // NH, S // T),
        in_specs=in_specs,
        out_specs=pl.BlockSpec((T, NH * LANES), lambda hp, i: (i, hp)),
        scratch_shapes=[pltpu.VMEM((NH, 8, LANES), F32)] if alibi else [],
        compiler_params=_params("parallel", "arbitrary"),
        name=name,
    )(*args)


def _window_body(T, W, NH, og_row0, q_ref, k_ref, v_ref, al_ref, sl_ref, omul_ref, og_ref, o_ref):
    i = pl.program_id(1)
    hs = range(NH)
    span = W + T
    q0 = i * T
    start = pl.multiple_of(jnp.maximum(q0 - W, 0), T)
    k = k_ref[pl.ds(start, span), :]
    dn = (((1,), (1,)), ((), ()))
    ss = [lax.dot_general(q_ref[:, _head_lanes(n)], k, dn, preferred_element_type=F32) for n in hs]
    off = (start - q0).astype(F32)
    rows = lax.broadcasted_iota(jnp.int32, (T, span), 0)
    cols = lax.broadcasted_iota(jnp.int32, (T, span), 1)
    dist = rows - cols + (q0 - start)
    ok = (dist >= 0) & (dist < W)
    ss = [jnp.where(ok, ss[n] + (al_ref[n] + sl_ref[n] * off), NEG_BIG) for n in hs]
    ps = [jnp.exp2(s - jnp.max(s, axis=-1, keepdims=True)) for s in ss]
    ls = [jnp.sum(p, axis=-1, keepdims=True) for p in ps]
    v = v_ref[pl.ds(start, span), :]
    os_ = [jnp.dot(p.astype(BF16), v, preferred_element_type=F32) for p in ps]
    og = og_ref[...]
    outs = [(os_[n] / ls[n] * omul_ref[:, _head_lanes(n)].astype(F32)
             * _lane_col(og, og_row0 + pl.program_id(0) * NH + n)).astype(o_ref.dtype) for n in hs]
    o_ref[...] = _cat(outs)


def window_attention(q, k, v, omul, og, og_row0, *, group, T, NH=4, W=NSA_WINDOW):
    S = q[0].shape[0]
    span = W + T
    assert W % T == 0 and S >= span and group % NH == 0
    al, sl = _alibi_rows(span)
    vec = pl.BlockSpec((NH, 1, span), lambda hp, i: (hp, 0, 0))
    return pl.pallas_call(
        functools.partial(_window_body, T, W, NH, og_row0),
        out_shape=jax.ShapeDtypeStruct((S, N_HEADS * LANES), BF16),
        grid=(N_HEADS // NH, S // T),
        in_specs=[_slab_spec(T, q, _same, NH * LANES), _shared_spec(k, "group", NH, group),
                  _shared_spec(v, "group", NH, group), vec, vec, _slab_spec(T, omul, _same, NH * LANES),
                  pl.BlockSpec((T, LANES), lambda hp, i: (i, 0))],
        out_specs=pl.BlockSpec((T, NH * LANES), lambda hp, i: (i, hp)),
        compiler_params=_params("parallel", "arbitrary"),
        name="nsa_window",
    )(q[0], k[0], v[0], al, sl, omul[0], og)


def _block_onehot(S, block):
    e = (np.arange(S)[:, None] // block) == np.arange(LANES)[None, :]
    return jnp.asarray(e.astype(np.float32), dtype=BF16)


def _take_top(score, col, n):
    colf = col.astype(F32)
    left = score
    for _ in range(n):
        m = jnp.max(left, axis=-1, keepdims=True)
        idx = jnp.min(jnp.where(left == m, colf, float(LANES)), axis=-1, keepdims=True)
        left = jnp.where(colf == idx, -jnp.inf, left)
    return (left == -jnp.inf) & (score > -jnp.inf)


def _moba_select_body(T, NB, q_ref, k_ref, o_ref, km_ref):
    i = pl.program_id(1)

    @pl.when(i == 0)
    def _():
        km_ref[...] = jnp.zeros_like(km_ref)
        for b in range(NB):
            kb = k_ref[b * MOBA_BLOCK:(b + 1) * MOBA_BLOCK, :].astype(F32)
            km_ref[b:b + 1, :] = jnp.sum(kb, axis=0, keepdims=True) * (1.0 / MOBA_BLOCK)

    q = q_ref[...]
    km = km_ref[...]
    km_hi = km.astype(BF16)
    km_lo = (km - km_hi.astype(F32)).astype(BF16)
    dn = (((1,), (1,)), ((), ()))
    score = (lax.dot_general(q, km_hi, dn, preferred_element_type=F32)
             + lax.dot_general(q, km_lo, dn, preferred_element_type=F32))
    col = lax.broadcasted_iota(jnp.int32, (T, LANES), 1)
    row = lax.broadcasted_iota(jnp.int32, (T, LANES), 0)
    q_blk = (i * T + row) // MOBA_BLOCK
    score = jnp.where(col < q_blk, score, -jnp.inf)
    taken = _take_top(score, col, min(MOBA_TOPK, NB)) | (col == q_blk)
    o_ref[...] = jnp.where(taken, 0.0, NEG_BIG).astype(o_ref.dtype)


def moba_select(q, k, T=1024):
    S = q[0].shape[0]
    NB = S // MOBA_BLOCK
    return pl.pallas_call(
        functools.partial(_moba_select_body, T, NB),
        out_shape=jax.ShapeDtypeStruct((N_HEADS, S, LANES), BF16),
        grid=(N_HEADS, S // T),
        in_specs=[_slab_spec(T, q, _same), _slab_spec(None, k, _same)],
        out_specs=pl.BlockSpec((None, T, LANES), lambda h, i: (h, i, 0)),
        scratch_shapes=[pltpu.VMEM((LANES, LANES), F32)],
        compiler_params=_params("parallel", "arbitrary"),
        name="moba_select",
    )(q[0], k[0])


def _softplus(z):
    return jnp.maximum(z, 0.0) + jnp.log(1.0 + jnp.exp(-jnp.abs(z)))


def _stick_body(T, NH, q_ref, k_ref, v_ref, omul_ref, o_ref):
    i = pl.program_id(1)
    rows = lax.broadcasted_iota(jnp.int32, (T, T), 0)
    cols = lax.broadcasted_iota(jnp.int32, (T, T), 1)
    later = jnp.where(rows > cols, 1.0, 0.0).astype(BF16)
    before = cols < rows
    hs = range(NH)
    qs = [q_ref[:, _head_lanes(n)] for n in hs]
    dn = (((1,), (1,)), ((), ()))

    def tiles(j, state, diag):
        start = pl.multiple_of(j * T, T)
        zs = [lax.dot_general(qs[n], k_ref[pl.ds(start, T), _head_lanes(n)], dn,
                              preferred_element_type=F32) for n in hs]
        log_keeps = [-_softplus(z) for z in zs]
        if diag:
            log_keeps = [jnp.where(before, lk, 0.0) for lk in log_keeps]
        his = [lk.astype(BF16) for lk in log_keeps]
        los = [(lk - hi.astype(F32)).astype(BF16) for lk, hi in zip(log_keeps, his)]
        b_his = [jnp.dot(hi, later, preferred_element_type=F32) for hi in his]
        b_los = [jnp.dot(lo, later, preferred_element_type=F32) for lo in los]
        aa = [jnp.exp(zs[n] + log_keeps[n] + (b_his[n] + b_los[n]) + state[n][0]) for n in hs]
        if diag:
            aa = [jnp.where(before, a, 0.0) for a in aa]
        accs = [state[n][1] + jnp.dot(aa[n].astype(BF16), v_ref[pl.ds(start, T), _head_lanes(n)],
                                      preferred_element_type=F32) for n in hs]
        runs = [state[n][0] + jnp.sum(log_keeps[n], axis=-1, keepdims=True) for n in hs]
        return tuple(zip(runs, accs))

    state = tiles(i, tuple((jnp.zeros((T, 1), F32), jnp.zeros((T, LANES), F32)) for _ in hs), True)

    def cond(st):
        j, state = st
        top = state[0][0]
        for run, _ in state[1:]:
            top = jnp.maximum(top, run)
        return jnp.logical_and(j >= 0, jnp.max(top) > SB_LOG_CUTOFF)

    def step(st):
        j, state = st
        return j - 1, tiles(j, state, False)

    _, state = lax.while_loop(cond, step, (i - 1, state))
    o = _cat([acc for _, acc in state])
    o_ref[...] = (o * omul_ref[...].astype(F32)).astype(o_ref.dtype)


def stick_attention(q, k, v, omul, T, NH=STICK_HEADS):
    S = q[0].shape[0]
    width = NH * LANES
    grid = (N_HEADS // NH, S // T)
    return pl.pallas_call(
        functools.partial(_stick_body, T, NH),
        out_shape=jax.ShapeDtypeStruct((S, MIX_WIDTH), BF16),
        grid=grid,
        in_specs=[_slab_spec(T, q, _same, width), _slab_spec(None, k, _same, width),
                  _slab_spec(None, v, _same, width), _slab_spec(T, omul, _same, width)],
        out_specs=pl.BlockSpec((T, width), lambda h, i: (i, h)),
        compiler_params=_params("parallel", "arbitrary"),
        name="stick_attention",
    )(q[0], k[0], v[0], omul[0])


def _nsa_compress_body(norm, x_ref, w_ref, pos_ref, g_ref, o_ref):
    half = NSA_CMP_STRIDE * HEAD_DIM
    x = x_ref[...]
    first = jnp.dot(x, w_ref[:half, :], preferred_element_type=F32)
    second = jnp.dot(x, w_ref[half:, :], preferred_element_type=F32)
    bias = jnp.dot(pos_ref[...], w_ref[...], preferred_element_type=F32)
    n = x.shape[0]
    y = first + pltpu.roll(second, n - 1, axis=0) + bias[0:1, :]
    if norm:
        y = _head_rms(y, g_ref[...], HEAD_DIM)
    o_ref[...] = y.astype(o_ref.dtype)


def nsa_compress(x, w, pos, gain, norm):
    G, n, width = x.shape
    pos_flat = jnp.broadcast_to(pos.reshape(1, -1), (16, pos.size)).astype(BF16)
    return pl.pallas_call(
        functools.partial(_nsa_compress_body, norm),
        out_shape=jax.ShapeDtypeStruct((G, n, HEAD_DIM), BF16),
        grid=(G,),
        in_specs=[pl.BlockSpec((None, n, width), lambda g: (g, 0, 0)),
                  pl.BlockSpec(w.shape, lambda g: (0, 0)),
                  pl.BlockSpec((16, pos.size), lambda g: (0, 0)),
                  pl.BlockSpec((1, HEAD_DIM), lambda g: (0, 0))],
        out_specs=pl.BlockSpec((None, n, HEAD_DIM), lambda g: (g, 0, 0)),
        compiler_params=_params("parallel"),
        name="nsa_compress",
    )(x, w.astype(BF16), pos_flat, gain.reshape(1, HEAD_DIM))


def _nsa_cmp_body(T, n_cmp, n_sel, q_ref, kc_ref, vc_ref, c2s_ref, sl_ref, omul_ref, og_ref,
                  o_ref, sel_ref):
    g, i = pl.program_id(0), pl.program_id(1)
    NC = kc_ref.shape[0]
    kc = kc_ref[...]
    c2s = c2s_ref[...]
    vc_c2s = jnp.concatenate([vc_ref[...], c2s], axis=-1)
    og = og_ref[...]
    row = lax.broadcasted_iota(jnp.int32, (T, NC), 0)
    col = lax.broadcasted_iota(jnp.int32, (T, NC), 1)
    dist = i * T + row - (col * NSA_CMP_STRIDE + (NSA_CMP_BLOCK - 1))
    ok = (dist >= 0) & (col < n_cmp)
    dist_f = dist.astype(F32)
    imp = jnp.zeros((T, LANES), F32)
    outs = []
    for p in range(NSA_HPG):
        qp = q_ref[:, p * LANES:(p + 1) * LANES]
        s = lax.dot_general(qp, kc, (((1,), (1,)), ((), ())), preferred_element_type=F32)
        s = jnp.where(ok, s - sl_ref[p] * dist_f, NEG_BIG)
        m = jnp.max(s, axis=-1, keepdims=True)
        e = jnp.exp2(s - m)
        l = jnp.sum(e, axis=-1, keepdims=True)
        inv = jnp.where(m > 0.5 * NEG_BIG, 1.0 / l, 0.0)
        prob = e * inv
        hi = prob.astype(BF16)
        lo = (prob - hi.astype(F32)).astype(BF16)
        both = jnp.dot(hi, vc_c2s, preferred_element_type=F32)
        o = both[:, :LANES]
        imp = imp + (both[:, LANES:] + jnp.dot(lo, c2s, preferred_element_type=F32))
        o = o * _lane_col(og, g * NSA_HPG + p) * omul_ref[:, p * LANES:(p + 1) * LANES].astype(F32)
        outs.append(o.astype(o_ref.dtype))
    o_ref[...] = jnp.concatenate(outs, axis=-1)

    blk = lax.broadcasted_iota(jnp.int32, (T, LANES), 1)
    cur = (i * T + lax.broadcasted_iota(jnp.int32, (T, LANES), 0)) // NSA_SEL_BLOCK
    kn = min(NSA_SEL_N, n_sel)
    forced = ((blk == 0) | (blk == cur) | (blk == cur - 1)) & (blk <= cur)
    imp = jnp.where((blk > cur) | (blk >= n_sel) | forced, -jnp.inf, imp)
    taken = forced | _take_top(imp, blk, kn - 3)
    sel_ref[...] = jnp.where(taken, 0.0, NEG_BIG).astype(sel_ref.dtype)


def nsa_compressed(q, k_cmp, v_cmp, omul, og, T):
    S = q[0].shape[0]
    G, NC, _ = k_cmp.shape
    n_cmp = (S - NSA_CMP_BLOCK) // NSA_CMP_STRIDE + 1
    n_sel = S // NSA_SEL_BLOCK
    assert 3 <= n_sel <= LANES and NC >= n_cmp
    c_start = np.arange(NC)[:, None] * NSA_CMP_STRIDE
    s_start = np.arange(LANES)[None, :] * NSA_SEL_BLOCK
    c2s = ((c_start < s_start + NSA_SEL_BLOCK) & (c_start + NSA_CMP_BLOCK > s_start)
           & (np.arange(NC)[:, None] < n_cmp) & (np.arange(LANES)[None, :] < n_sel))
    slopes = np.broadcast_to(_alibi_slopes()[:, None, None] * LOG2E, (N_HEADS, 1, NC)).astype(np.float32)
    gw = NSA_HPG * LANES
    cmp_spec = pl.BlockSpec((None, NC, HEAD_DIM), lambda g, i: (g, 0, 0))
    return pl.pallas_call(
        functools.partial(_nsa_cmp_body, T, n_cmp, n_sel),
        out_shape=[jax.ShapeDtypeStruct((S, MIX_WIDTH), BF16),
                   jax.ShapeDtypeStruct((G, S, LANES), BF16)],
        grid=(G, S // T),
        in_specs=[_slab_spec(T, q, _same, gw), cmp_spec, cmp_spec,
                  pl.BlockSpec((NC, LANES), lambda g, i: (0, 0)),
                  pl.BlockSpec((NSA_HPG, 1, NC), lambda g, i: (g, 0, 0)),
                  _slab_spec(T, omul, _same, gw),
                  pl.BlockSpec((T, LANES), lambda g, i: (i, 0))],
        out_specs=[pl.BlockSpec((T, gw), lambda g, i: (i, g)),
                   pl.BlockSpec((None, T, LANES), lambda g, i: (g, i, 0))],
        compiler_params=_params("parallel", "parallel"),
        name="nsa_compressed",
    )(q[0], k_cmp, v_cmp, jnp.asarray(c2s.astype(np.float32), dtype=BF16), jnp.asarray(slopes),
      omul[0], og)


def _row(v):
    return v.reshape(1, -1).astype(F32)


def _rope_tables(S):
    inv_freq = ROPE_THETA ** (-jnp.arange(0, MLA_ROPE, 2, dtype=F32) / MLA_ROPE)
    ang = jnp.arange(S, dtype=F32)[:, None] * inv_freq[None, :]
    cos, sin = jnp.cos(ang), jnp.sin(ang)
    z = jnp.zeros_like(cos)
    return (jnp.concatenate([cos, z, cos, z], axis=-1),
            jnp.concatenate([-sin, z, sin, z], axis=-1))


def _rope_lanes(v):
    half = MLA_ROPE // 2
    z = jnp.zeros(v.shape[:-1] + (half,), v.dtype)
    return jnp.concatenate([v[..., :half], z, v[..., half:], z], axis=-1)


def mla_mixer(h, w_in, q_lat_norm, w_q_up, kv_lat_norm, w_kv_up, q_norm, k_norm):
    S = h.shape[0]
    c1 = MLA_Q_RANK
    c2 = c1 + MLA_KV_RANK
    c3 = c2 + MLA_ROPE
    scale = MLA_QK_DIM ** -0.5 * LOG2E
    cos, sin = _rope_tables(S)
    tables = [("cos", cos, "row"), ("sin", sin, "row")]

    w_lat = jnp.concatenate([w_in[:, :c2], _rope_lanes(w_in[:, c2:c3]),
                             jnp.zeros((w_in.shape[0], PROJ_TN - c2 - LANES), w_in.dtype)], axis=-1)
    lat = proj(h, w_lat, [(0, 1, _epi_mla_in)],
               [("g_qlat", _row(q_lat_norm), "vec"), ("g_kvlat", _row(kv_lat_norm), "vec"),
                ("gk_r", _row(_rope_lanes(k_norm[MLA_NOPE:])), "vec")] + tables, name="mla_latents")
    gate = proj(h, w_in[:, c3:], [(0, 2, _epi_silu)], name="mla_gate")

    wq = w_q_up.reshape(MLA_Q_RANK, N_HEADS, MLA_QK_DIM)
    wq = jnp.concatenate([wq[..., :MLA_NOPE], _rope_lanes(wq[..., MLA_NOPE:])], axis=-1)
    q = proj(lat, wq.reshape(MLA_Q_RANK, N_HEADS * 2 * LANES).astype(BF16), [(0, 4, _epi_mla_q(scale))],
             [("gq_n", _row(q_norm[:MLA_NOPE]), "vec"), ("gq_r", _row(_rope_lanes(q_norm[MLA_NOPE:])), "vec")]
             + tables, name="mla_q")
    wkv = w_kv_up.reshape(MLA_KV_RANK, N_HEADS, MLA_NOPE + HEAD_DIM)
    wkv = jnp.concatenate([wkv[..., :MLA_NOPE].reshape(MLA_KV_RANK, -1),
                           wkv[..., MLA_NOPE:].reshape(MLA_KV_RANK, -1)], axis=-1).astype(BF16)
    kv = proj(lat, wkv, [(0, 2, _epi_headnorm("gk_n")), (2, 4, _epi_scale(1.0))],
              [("gk_n", _row(k_norm[:MLA_NOPE]), "vec")], h_block=c1 // MLA_KV_RANK, name="mla_kv")
    k_rope = (lat, c2 // LANES)
    return [flash((q, 0), [((kv, 0), "head"), (k_rope, "all")], ((kv, N_HEADS), "head"), (gate, 0),
                  dq=2 * LANES, T=min(FLASH_T, S), name="mla_attention")]


def moba_mixer(h, w_in, q_norm, k_norm):
    S = h.shape[0]
    scale = HEAD_DIM ** -0.5 * LOG2E
    n = MIX_WIDTH // PROJ_TN
    qkvg = proj(h, w_in, [(0, n, _epi_headnorm("gq", scale)), (n, 2 * n, _epi_headnorm("gk")),
                          (2 * n, 3 * n, _epi_scale(1.0)), (3 * n, 4 * n, _epi_silu)],
                [("gq", _row(q_norm), "vec"), ("gk", _row(k_norm), "vec")], name="moba_in")
    H = N_HEADS
    q, k, v, gate = (qkvg, 0), (qkvg, H), (qkvg, 2 * H), (qkvg, 3 * H)
    bias = moba_select(q, k)
    return [flash(q, [(k, "head"), ((_block_onehot(S, MOBA_BLOCK), 0), "all")], (v, "head"), gate, dq=LANES,
                  qx=(bias, "head"), alibi=True, T=min(FLASH_T, S), name="moba_attention")]


def stick_mixer(h, w_in):
    S = h.shape[0]
    scale = HEAD_DIM ** -0.5
    n = MIX_WIDTH // PROJ_TN
    qkvg = proj(h, w_in, [(0, n, _epi_scale(scale)), (n, 3 * n, _epi_scale(1.0)), (3 * n, 4 * n, _epi_silu)],
                name="stick_in")
    H = N_HEADS
    return [stick_attention((qkvg, 0), (qkvg, H), (qkvg, 2 * H), (qkvg, 3 * H), T=min(STICK_T, S))]


def nsa_mixer(h, w_in, q_norm, k_norm, w_cmp_k, w_cmp_v, cmp_pos):
    S = h.shape[0]
    G, P, H = NSA_GROUPS, NSA_HPG, N_HEADS
    W, KV = MIX_WIDTH, NSA_KV_WIDTH
    half = KV // 2
    scale = HEAD_DIM ** -0.5 * LOG2E
    cuts = np.cumsum([W, KV, KV, KV, 3 * H]).tolist()
    assert KV == PROJ_TN and W % PROJ_TN == 0
    n = W // PROJ_TN
    main = proj(h, w_in, [(0, n, _epi_headnorm("gq", scale)), (n, n + 1, _epi_scale(1.0)),
                          (n + 1, n + 2, _epi_headnorm("gk1", n_norm=half)),
                          (n + 2, n + 3, _epi_headnorm("gk2", n_norm=half))],
                [("gq", _row(q_norm), "vec"), ("gk1", _row(k_norm[1]), "vec"), ("gk2", _row(k_norm[2]), "vec")],
                n_tiles=n + 3, name="nsa_in")
    q = (main, 0)
    kc0 = H
    ks, vs = (main, kc0 + 2 * G), (main, kc0 + 3 * G)
    kw, vw = (main, kc0 + 4 * G), (main, kc0 + 5 * G)
    wg = w_in[:, cuts[3]:cuts[4]].reshape(-1, H, 3).transpose(0, 2, 1).reshape(-1, 3 * H)
    wg = jnp.pad(wg, ((0, 0), (0, LANES - 3 * H)))
    og = proj(h, wg.astype(BF16), [(0, 1, _epi_sigmoid)], tn=LANES, out_dtype=F32, name="nsa_branch_gates")
    gate = (proj(h, w_in[:, cuts[4]:], [(0, n, _epi_silu)], name="nsa_gate"), 0)

    def chunks(t):
        return t.reshape(S // NSA_CMP_STRIDE, NSA_CMP_STRIDE, G, HEAD_DIM).transpose(2, 0, 1, 3).reshape(
            G, S // NSA_CMP_STRIDE, NSA_CMP_STRIDE * HEAD_DIM)

    c0 = kc0 * LANES
    k_cmp = nsa_compress(chunks(main[:, c0:c0 + half]), w_cmp_k, cmp_pos, k_norm[0], True)
    v_cmp = nsa_compress(chunks(main[:, c0 + half:c0 + KV]), w_cmp_v, cmp_pos, k_norm[0], False)
    o_cmp, bias = nsa_compressed(q, k_cmp, v_cmp, gate, og, T=min(NSA_CMP_T, S))
    o_slc = flash(q, [(ks, "group"), ((_block_onehot(S, NSA_SEL_BLOCK), 0), "all")], (vs, "group"), gate,
                  dq=LANES, group=P, qx=(bias, "group"), alibi=True, og=og, og_row0=H,
                  T=min(FLASH_T, S), name="nsa_selected")
    o_win = window_attention(q, kw, vw, gate, og, 2 * H, group=P, T=min(WINDOW_T, S))
    return [o_cmp, o_slc, o_win]


def kernel(x, norm_a, w_in_a, q_lat_norm_a, w_q_up_a, kv_lat_norm_a, w_kv_up_a, q_norm_a, k_norm_a, w_out_a,
           norm_b, w_in_b, q_norm_b, k_norm_b, w_out_b,
           norm_c, w_in_c, w_out_c,
           norm_d, w_in_d, q_norm_d, k_norm_d, w_cmp_k_d, w_cmp_v_d, cmp_pos_d, w_out_d):
    B, S, D = x.shape
    norms = (norm_a, norm_b, norm_c, norm_d)
    w_outs = (w_out_a, w_out_b, w_out_c, w_out_d)
    depth = norm_a.shape[0] + norm_b.shape[0] + norm_c.shape[0] + norm_d.shape[0]
    rows = x.reshape(B * S, D)
    outs = []
    for b in range(B):
        xb = rows if B == 1 else rows[b * S:(b + 1) * S]
        h = rmsnorm_rows(xb, norms[0][0])
        for layer in range(depth):
            m, j = layer % 4, layer // 4
            if m == 0:
                a = mla_mixer(h, w_in_a[j], q_lat_norm_a[j], w_q_up_a[j], kv_lat_norm_a[j], w_kv_up_a[j],
                              q_norm_a[j], k_norm_a[j])
            elif m == 1:
                a = moba_mixer(h, w_in_b[j], q_norm_b[j], k_norm_b[j])
            elif m == 2:
                a = stick_mixer(h, w_in_c[j])
            else:
                a = nsa_mixer(h, w_in_d[j], q_norm_d[j], k_norm_d[j], w_cmp_k_d[j], w_cmp_v_d[j], cmp_pos_d[j])
            nxt = layer + 1
            g_next = norms[nxt % 4][nxt // 4] if nxt < depth else None
            xb, h = outproj(a, w_outs[m][j].astype(BF16), xb, g_next)
        outs.append(xb)
    out = outs[0] if B == 1 else jnp.concatenate(outs, axis=0)
    return out.reshape(B, S, D)
```

```python
import functools

import numpy as np
import jax
import jax.numpy as jnp
from jax import lax
from jax.experimental import pallas as pl
from jax.experimental.pallas import tpu as pltpu

N_HEADS = 16
HEAD_DIM = 128
MIX_WIDTH = N_HEADS * HEAD_DIM
EPS = 1e-6
NEG_BIG = -1e30
LANES = 128

MLA_Q_RANK = 512
MLA_KV_RANK = 256
MLA_NOPE = 128
MLA_ROPE = 64
MLA_QK_DIM = MLA_NOPE + MLA_ROPE
ROPE_THETA = 10000.0

MOBA_BLOCK = 256
MOBA_TOPK = 3

NSA_GROUPS = 4
NSA_HPG = N_HEADS // NSA_GROUPS
NSA_CMP_BLOCK = 32
NSA_CMP_STRIDE = 16
NSA_SEL_BLOCK = 64
NSA_SEL_N = 16
NSA_WINDOW = 512
NSA_KV_WIDTH = 2 * NSA_GROUPS * HEAD_DIM

SB_LOG_CUTOFF = -110.0
EXP2_ZERO = 151.0

VMEM_LIMIT = 56 * 1024 * 1024

LOG2E = 1.4426950408889634

FLASH_T = 1024
WINDOW_T = 512
STICK_T = 256
NSA_CMP_T = 1024
FLASH_HEADS = 2
STICK_HEADS = 4
PROJ_TM = 1024
PROJ_TN = 1024
OUTPROJ_TM = 512

BF16 = jnp.bfloat16
F32 = jnp.float32


def _params(*sem):
    return pltpu.CompilerParams(dimension_semantics=sem, vmem_limit_bytes=VMEM_LIMIT)


def _alibi_slopes():
    return 2.0 ** (-8.0 * np.arange(1, N_HEADS + 1) / N_HEADS)


def _rmsnorm_body(x_ref, g_ref, o_ref):
    x = x_ref[...]
    ms = jnp.mean(x * x, axis=-1, keepdims=True)
    o_ref[...] = (x * lax.rsqrt(ms + EPS) * g_ref[...]).astype(o_ref.dtype)


def rmsnorm_rows(x, g, tm=512):
    S, D = x.shape
    return pl.pallas_call(
        _rmsnorm_body,
        out_shape=jax.ShapeDtypeStruct((S, D), BF16),
        grid=(S // tm,),
        in_specs=[pl.BlockSpec((tm, D), lambda i: (i, 0)),
                  pl.BlockSpec((1, D), lambda i: (0, 0))],
        out_specs=pl.BlockSpec((tm, D), lambda i: (i, 0)),
        compiler_params=_params("parallel"),
        name="rmsnorm_rows",
    )(x, g.reshape(1, D))


def _head_rms(y, gain, n_real):
    ms = jnp.sum(y * y, axis=-1, keepdims=True) * (1.0 / n_real)
    return y * lax.rsqrt(ms + EPS) * gain


def _cat(parts):
    return parts[0] if len(parts) == 1 else jnp.concatenate(parts, axis=-1)


def _epi_scale(scale):
    def epi(acc, ex):
        return acc * scale if scale != 1.0 else acc
    return epi


def _epi_silu(acc, ex):
    return acc / (1.0 + jnp.exp(-acc))


def _epi_sigmoid(acc, ex):
    return 1.0 / (1.0 + jnp.exp(-acc))


def _epi_headnorm(gain, scale=1.0, n_norm=None):
    def epi(acc, ex):
        n = acc.shape[-1] if n_norm is None else n_norm
        g = ex[gain][...] * scale
        parts = [_head_rms(acc[:, c:c + LANES], g, LANES) for c in range(0, n, LANES)]
        if n < acc.shape[-1]:
            parts.append(acc[:, n:])
        return _cat(parts)
    return epi


def _rope_slab(y, gain, cos, sin, scale):
    y = _head_rms(y, gain, MLA_ROPE)
    return (y * cos + pltpu.roll(y, 64, axis=1) * sin) * scale


def _epi_mla_q(scale):
    def epi(acc, ex):
        gn = ex["gq_n"][...] * scale
        gr, cos, sin = ex["gq_r"][...], ex["cos"][...], ex["sin"][...]
        parts = []
        for c in range(0, acc.shape[-1], 2 * LANES):
            parts.append(_head_rms(acc[:, c:c + LANES], gn, MLA_NOPE))
            parts.append(_rope_slab(acc[:, c + LANES:c + 2 * LANES], gr, cos, sin, scale))
        return _cat(parts)
    return epi


def _epi_mla_in(acc, ex):
    c1, c2 = MLA_Q_RANK, MLA_Q_RANK + MLA_KV_RANK
    q_lat = _head_rms(acc[:, :c1], ex["g_qlat"][...], c1)
    kv_lat = _head_rms(acc[:, c1:c2], ex["g_kvlat"][...], MLA_KV_RANK)
    k_rope = _rope_slab(acc[:, c2:c2 + LANES], ex["gk_r"][...], ex["cos"][...], ex["sin"][...], 1.0)
    return _cat([q_lat, kv_lat, k_rope, jnp.zeros_like(k_rope)])


def _proj_body(epis, names, cast, h_ref, w_ref, *refs):
    n = len(names)
    ex = dict(zip(names, refs[:n]))
    o_ref = refs[n]
    j, i = pl.program_id(0), pl.program_id(1)
    if cast:
        wb_ref = refs[n + 1]

        @pl.when(i == 0)
        def _():
            wb_ref[...] = w_ref[...].astype(BF16)

        w = wb_ref[...]
    else:
        w = w_ref[...]
    acc = jnp.dot(h_ref[...], w, preferred_element_type=F32)
    if len(epis) == 1:
        o_ref[...] = epis[0][2](acc, ex).astype(o_ref.dtype)
    else:
        for t0, t1, epi in epis:
            @pl.when((j >= t0) & (j < t1))
            def _(epi=epi):
                o_ref[...] = epi(acc, ex).astype(o_ref.dtype)


def proj(h, w, epis, extra=(), *, h_block=0, w_tile0=0, n_tiles=None, tn=PROJ_TN, out_dtype=BF16, name="proj"):
    S = h.shape[0]
    K = w.shape[0]
    tm = PROJ_TM
    n_tiles = (w.shape[1] // tn - w_tile0) if n_tiles is None else n_tiles
    cast = w.dtype != BF16
    in_specs = [pl.BlockSpec((tm, K), lambda j, i: (i, h_block)),
                pl.BlockSpec((K, tn), lambda j, i: (0, w_tile0 + j))]
    args = [h, w]
    names = []
    for nm, arr, kind in extra:
        names.append(nm)
        if kind == "vec":
            in_specs.append(pl.BlockSpec(arr.shape, lambda j, i: (0, 0)))
        else:
            in_specs.append(pl.BlockSpec((tm, LANES), lambda j, i: (i, 0)))
        args.append(arr)
    return pl.pallas_call(
        functools.partial(_proj_body, tuple(epis), tuple(names), cast),
        out_shape=jax.ShapeDtypeStruct((S, n_tiles * tn), out_dtype),
        grid=(n_tiles, S // tm),
        in_specs=in_specs,
        out_specs=pl.BlockSpec((tm, tn), lambda j, i: (i, j)),
        scratch_shapes=[pltpu.VMEM((K, tn), BF16)] if cast else [],
        compiler_params=_params("arbitrary", "arbitrary"),
        name=name,
    )(*args)


def _outproj_body(n_a, has_next, *refs):
    a_refs = refs[:n_a]
    w_ref, x_ref = refs[n_a], refs[n_a + 1]
    if has_next:
        g_ref, xo_ref, ho_ref = refs[n_a + 2:n_a + 5]
    else:
        xo_ref = refs[n_a + 2]
    if n_a == 1:
        a = a_refs[0][...]
    else:
        a = a_refs[0][...].astype(F32)
        for r in a_refs[1:]:
            a = a + r[...].astype(F32)
        a = a.astype(BF16)
    xn = x_ref[...] + jnp.dot(a, w_ref[...], preferred_element_type=F32)
    xo_ref[...] = xn
    if has_next:
        ms = jnp.mean(xn * xn, axis=-1, keepdims=True)
        ho_ref[...] = (xn * lax.rsqrt(ms + EPS) * g_ref[...]).astype(BF16)


def outproj(a_list, w, x, g_next=None):
    S, D = x.shape
    n_a = len(a_list)
    tm = OUTPROJ_TM // 2 if n_a > 1 else OUTPROJ_TM
    has_next = g_next is not None
    row = pl.BlockSpec((tm, D), lambda i: (i, 0))
    in_specs = [pl.BlockSpec((tm, MIX_WIDTH), lambda i: (i, 0)) for _ in a_list]
    in_specs += [pl.BlockSpec((MIX_WIDTH, D), lambda i: (0, 0)), row]
    args = list(a_list) + [w, x]
    out_shape = [jax.ShapeDtypeStruct((S, D), F32)]
    out_specs = [row]
    if has_next:
        in_specs.append(pl.BlockSpec((1, D), lambda i: (0, 0)))
        args.append(g_next.reshape(1, D))
        out_shape.append(jax.ShapeDtypeStruct((S, D), BF16))
        out_specs.append(row)
    res = pl.pallas_call(
        functools.partial(_outproj_body, n_a, has_next),
        out_shape=out_shape,
        grid=(S // tm,),
        in_specs=in_specs,
        out_specs=out_specs,
        compiler_params=_params("parallel"),
        name="outproj",
    )(*args)
    return (res[0], res[1]) if has_next else (res[0], None)


def _lane_col(x, c):
    lane = lax.broadcasted_iota(jnp.int32, x.shape, 1)
    return jnp.sum(jnp.where(lane == c, x, 0.0), axis=-1, keepdims=True)


def _head_lanes(n):
    return slice(n * LANES, (n + 1) * LANES)


def _flash_body(T, NH, dq, k_modes, v_mode, qx_mode, has_alibi, og_row0, *refs):
    pos = 0
    q_ref = refs[pos]; pos += 1
    if qx_mode is not None:
        qx_ref = refs[pos]; pos += 1
    k_refs = refs[pos:pos + len(k_modes)]; pos += len(k_modes)
    v_ref = refs[pos]; pos += 1
    if has_alibi:
        al_ref, sl_ref = refs[pos], refs[pos + 1]; pos += 2
    omul_ref = refs[pos]; pos += 1
    if og_row0 is not None:
        og_ref = refs[pos]; pos += 1
    o_ref = refs[pos]

    i = pl.program_id(1)
    hs = range(NH)
    if has_alibi:
        kmax_ref = refs[pos + 1]

        @pl.when(i == 0)
        def _():
            for n in hs:
                kn = k_refs[0][:, _head_lanes(n)] if k_modes[0] == "head" else k_refs[0][...]
                kn = kn.astype(F32)
                k2 = jnp.sum(kn * kn, axis=-1, keepdims=True)
                kmax_ref[n] = jnp.broadcast_to(jnp.max(k2, axis=0, keepdims=True), kmax_ref.shape[1:])

    qs = []
    for n in hs:
        qn = q_ref[:, n * dq:(n + 1) * dq]
        if qx_mode == "head":
            qn = jnp.concatenate([qn, qx_ref[n]], axis=-1)
        elif qx_mode == "group":
            qn = jnp.concatenate([qn, qx_ref[...]], axis=-1)
        qs.append(qn)
    dn = (((1,), (1,)), ((), ()))

    def keys(start):
        parts = [kr[pl.ds(start, T), :] for kr in k_refs]
        return [_cat([p[:, _head_lanes(n)] if mode == "head" else p for p, mode in zip(parts, k_modes)])
                for n in hs]

    ones_slab = jnp.where(lax.broadcasted_iota(jnp.int32, (T, LANES), 1) == 0, 1.0, 0.0).astype(BF16)

    def values(start):
        v = v_ref[pl.ds(start, T), :]
        return [jnp.concatenate([v[:, _head_lanes(n)] if v_mode == "head" else v, ones_slab], axis=-1)
                for n in hs]

    def tile(j, carry):
        start = pl.multiple_of(j * T, T)
        ks = keys(start)
        ss = [lax.dot_general(qs[n], ks[n], dn, preferred_element_type=F32) for n in hs]
        if has_alibi:
            off = ((j - i) * T).astype(F32)
            ss = [ss[n] + (al_ref[n] + sl_ref[n] * off) for n in hs]
        ms = [jnp.maximum(carry[n][0], jnp.max(ss[n], axis=-1, keepdims=True)) for n in hs]
        alphas = [jnp.exp2(carry[n][0] - ms[n]) for n in hs]
        ps = [jnp.exp2(ss[n] - ms[n]) for n in hs]
        vs = values(start)
        accs = [alphas[n] * carry[n][1] + jnp.dot(ps[n].astype(BF16), vs[n], preferred_element_type=F32)
                for n in hs]
        return tuple(zip(ms, accs))

    def diagonal():
        start = pl.multiple_of(i * T, T)
        ks = keys(start)
        ss = [lax.dot_general(qs[n], ks[n], dn, preferred_element_type=F32) for n in hs]
        if has_alibi:
            ss = [ss[n] + al_ref[n] for n in hs]
        rows = lax.broadcasted_iota(jnp.int32, (T, T), 0)
        cols = lax.broadcasted_iota(jnp.int32, (T, T), 1)
        ss = [jnp.where(rows >= cols, s, NEG_BIG) for s in ss]
        ms = [jnp.max(s, axis=-1, keepdims=True) for s in ss]
        ps = [jnp.exp2(s - m) for s, m in zip(ss, ms)]
        vs = values(start)
        accs = [jnp.dot(ps[n].astype(BF16), vs[n], preferred_element_type=F32) for n in hs]
        return tuple(zip(ms, accs))

    carry = diagonal()
    if has_alibi:
        reach = None
        for n in hs:
            q2 = qs[n][:, :LANES].astype(F32)
            q2 = jnp.sum(q2 * q2, axis=-1, keepdims=True)
            bound = (2.0 * jnp.sqrt(q2 * jnp.max(kmax_ref[n])) + EXP2_ZERO) / jnp.max(sl_ref[n])
            reach = bound if reach is None else jnp.maximum(reach, bound)
        reach = jnp.max(reach)
    else:
        reach = jnp.inf

    def cond(st):
        j = st[0]
        return jnp.logical_and(j >= 0, ((i - 1 - j) * T + 1).astype(F32) <= reach)

    _, carry = lax.while_loop(cond, lambda st: (st[0] - 1, tile(st[0], st[1])), (i - 1, carry))
    outs = []
    for n in hs:
        acc = carry[n][1]
        o = acc[:, :LANES] / acc[:, LANES:LANES + 1] * omul_ref[:, _head_lanes(n)].astype(F32)
        if og_row0 is not None:
            o = o * _lane_col(og_ref[...], og_row0 + pl.program_id(0) * NH + n)
        outs.append(o.astype(o_ref.dtype))
    o_ref[...] = _cat(outs)


def _slab_spec(rows, src, head_of, width=LANES):
    off = src[1] * LANES // width
    if rows is None:
        return pl.BlockSpec((src[0].shape[0], width), lambda h, i: (0, off + head_of(h)))
    return pl.BlockSpec((rows, width), lambda h, i: (i, off + head_of(h)))


def _same(h):
    return h


def _first(h):
    return 0


def _shared_spec(src, mode, NH, group):
    if mode == "head":
        return _slab_spec(None, src, _same, NH * LANES)
    if mode == "group":
        return _slab_spec(None, src, lambda hp: (hp * NH) // group)
    return _slab_spec(None, src, _first)


def _alibi_rows(n):
    sl = _alibi_slopes()[:, None, None] * LOG2E
    al = (sl * np.arange(n)[None, None, :]).astype(np.float32)
    return jnp.asarray(al), jnp.asarray(np.broadcast_to(sl, (N_HEADS, 1, n)).astype(np.float32))


def flash(q, k_parts, v, omul, *, dq, T, NH=FLASH_HEADS, group=1, qx=None, alibi=False,
          og=None, og_row0=0, name="flash"):
    S = q[0].shape[0]
    H = N_HEADS
    assert H % NH == 0 and (group == 1 or group % NH == 0)
    in_specs = [_slab_spec(T, q, _same, NH * dq)]
    args = [q[0]]
    if qx is not None:
        if qx[1] == "head":
            in_specs.append(pl.BlockSpec((NH, T, LANES), lambda hp, i: (hp, i, 0)))
        else:
            in_specs.append(pl.BlockSpec((None, T, LANES), lambda hp, i: ((hp * NH) // group, i, 0)))
        args.append(qx[0])
    for src, mode in k_parts:
        in_specs.append(_shared_spec(src, mode, NH, group))
        args.append(src[0])
    in_specs.append(_shared_spec(v[0], v[1], NH, group))
    args.append(v[0][0])
    if alibi:
        in_specs += [pl.BlockSpec((NH, 1, T), lambda hp, i: (hp, 0, 0))] * 2
        args += list(_alibi_rows(T))
    in_specs.append(_slab_spec(T, omul, _same, NH * LANES))
    args.append(omul[0])
    if og is not None:
        in_specs.append(pl.BlockSpec((T, LANES), lambda hp, i: (i, 0)))
        args.append(og)
    body = functools.partial(_flash_body, T, NH, dq, tuple(m for _, m in k_parts), v[1],
                             None if qx is None else qx[1], alibi, None if og is None else og_row0)
    return pl.pallas_call(
        body,
        out_shape=jax.ShapeDtypeStruct((S, H * LANES), BF16),
        grid=(H // NH, S // T),
        in_specs=in_specs,
        out_specs=pl.BlockSpec((T, NH * LANES), lambda hp, i: (i, hp)),
        scratch_shapes=[pltpu.VMEM((NH, 8, LANES), F32)] if alibi else [],
        compiler_params=_params("parallel", "arbitrary"),
        name=name,
    )(*args)


def _window_body(T, W, NH, og_row0, q_ref, k_ref, v_ref, al_ref, sl_ref, omul_ref, og_ref, o_ref):
    i = pl.program_id(1)
    hs = range(NH)
    span = W + T
    q0 = i * T
    start = pl.multiple_of(jnp.maximum(q0 - W, 0), T)
    k = k_ref[pl.ds(start, span), :]
    dn = (((1,), (1,)), ((), ()))
    ss = [lax.dot_general(q_ref[:, _head_lanes(n)], k, dn, preferred_element_type=F32) for n in hs]
    off = (start - q0).astype(F32)
    rows = lax.broadcasted_iota(jnp.int32, (T, span), 0)
    cols = lax.broadcasted_iota(jnp.int32, (T, span), 1)
    dist = rows - cols + (q0 - start)
    ok = (dist >= 0) & (dist < W)
    ss = [jnp.where(ok, ss[n] + (al_ref[n] + sl_ref[n] * off), NEG_BIG) for n in hs]
    ps = [jnp.exp2(s - jnp.max(s, axis=-1, keepdims=True)) for s in ss]
    ls = [jnp.sum(p, axis=-1, keepdims=True) for p in ps]
    v = v_ref[pl.ds(start, span), :]
    os_ = [jnp.dot(p.astype(BF16), v, preferred_element_type=F32) for p in ps]
    og = og_ref[...]
    outs = [(os_[n] / ls[n] * omul_ref[:, _head_lanes(n)].astype(F32)
             * _lane_col(og, og_row0 + pl.program_id(0) * NH + n)).astype(o_ref.dtype) for n in hs]
    o_ref[...] = _cat(outs)


def window_attention(q, k, v, omul, og, og_row0, *, group, T, NH=4, W=NSA_WINDOW):
    S = q[0].shape[0]
    span = W + T
    assert W % T == 0 and S >= span and group % NH == 0
    al, sl = _alibi_rows(span)
    vec = pl.BlockSpec((NH, 1, span), lambda hp, i: (hp, 0, 0))
    return pl.pallas_call(
        functools.partial(_window_body, T, W, NH, og_row0),
        out_shape=jax.ShapeDtypeStruct((S, N_HEADS * LANES), BF16),
        grid=(N_HEADS // NH, S // T),
        in_specs=[_slab_spec(T, q, _same, NH * LANES), _shared_spec(k, "group", NH, group),
                  _shared_spec(v, "group", NH, group), vec, vec, _slab_spec(T, omul, _same, NH * LANES),
                  pl.BlockSpec((T, LANES), lambda hp, i: (i, 0))],
        out_specs=pl.BlockSpec((T, NH * LANES), lambda hp, i: (i, hp)),
        compiler_params=_params("parallel", "arbitrary"),
        name="nsa_window",
    )(q[0], k[0], v[0], al, sl, omul[0], og)


def _block_onehot(S, block):
    e = (np.arange(S)[:, None] // block) == np.arange(LANES)[None, :]
    return jnp.asarray(e.astype(np.float32), dtype=BF16)


def _take_top(score, col, n):
    colf = col.astype(F32)
    left = score
    for _ in range(n):
        m = jnp.max(left, axis=-1, keepdims=True)
        idx = jnp.min(jnp.where(left == m, colf, float(LANES)), axis=-1, keepdims=True)
        left = jnp.where(colf == idx, -jnp.inf, left)
    return (left == -jnp.inf) & (score > -jnp.inf)


def _moba_select_body(T, NB, q_ref, k_ref, o_ref, km_ref):
    i = pl.program_id(1)

    @pl.when(i == 0)
    def _():
        km_ref[...] = jnp.zeros_like(km_ref)
        for b in range(NB):
            kb = k_ref[b * MOBA_BLOCK:(b + 1) * MOBA_BLOCK, :].astype(F32)
            km_ref[b:b + 1, :] = jnp.sum(kb, axis=0, keepdims=True) * (1.0 / MOBA_BLOCK)

    q = q_ref[...]
    km = km_ref[...]
    km_hi = km.astype(BF16)
    km_lo = (km - km_hi.astype(F32)).astype(BF16)
    dn = (((1,), (1,)), ((), ()))
    score = (lax.dot_general(q, km_hi, dn, preferred_element_type=F32)
             + lax.dot_general(q, km_lo, dn, preferred_element_type=F32))
    col = lax.broadcasted_iota(jnp.int32, (T, LANES), 1)
    row = lax.broadcasted_iota(jnp.int32, (T, LANES), 0)
    q_blk = (i * T + row) // MOBA_BLOCK
    score = jnp.where(col < q_blk, score, -jnp.inf)
    taken = _take_top(score, col, min(MOBA_TOPK, NB)) | (col == q_blk)
    o_ref[...] = jnp.where(taken, 0.0, NEG_BIG).astype(o_ref.dtype)


def moba_select(q, k, T=2048):
    S = q[0].shape[0]
    NB = S // MOBA_BLOCK
    return pl.pallas_call(
        functools.partial(_moba_select_body, T, NB),
        out_shape=jax.ShapeDtypeStruct((N_HEADS, S, LANES), BF16),
        grid=(N_HEADS, S // T),
        in_specs=[_slab_spec(T, q, _same), _slab_spec(None, k, _same)],
        out_specs=pl.BlockSpec((None, T, LANES), lambda h, i: (h, i, 0)),
        scratch_shapes=[pltpu.VMEM((LANES, LANES), F32)],
        compiler_params=_params("parallel", "arbitrary"),
        name="moba_select",
    )(q[0], k[0])


def _softplus(z):
    return jnp.maximum(z, 0.0) + jnp.log(1.0 + jnp.exp(-jnp.abs(z)))


def _stick_body(T, NH, q_ref, k_ref, v_ref, omul_ref, o_ref):
    i = pl.program_id(1)
    rows = lax.broadcasted_iota(jnp.int32, (T, T), 0)
    cols = lax.broadcasted_iota(jnp.int32, (T, T), 1)
    later = jnp.where(rows > cols, 1.0, 0.0).astype(BF16)
    before = cols < rows
    hs = range(NH)
    qs = [q_ref[:, _head_lanes(n)] for n in hs]
    dn = (((1,), (1,)), ((), ()))

    def tiles(j, state, diag):
        start = pl.multiple_of(j * T, T)
        zs = [lax.dot_general(qs[n], k_ref[pl.ds(start, T), _head_lanes(n)], dn,
                              preferred_element_type=F32) for n in hs]
        log_keeps = [-_softplus(z) for z in zs]
        if diag:
            log_keeps = [jnp.where(before, lk, 0.0) for lk in log_keeps]
        his = [lk.astype(BF16) for lk in log_keeps]
        los = [(lk - hi.astype(F32)).astype(BF16) for lk, hi in zip(log_keeps, his)]
        b_his = [jnp.dot(hi, later, preferred_element_type=F32) for hi in his]
        b_los = [jnp.dot(lo, later, preferred_element_type=F32) for lo in los]
        aa = [jnp.exp(zs[n] + log_keeps[n] + (b_his[n] + b_los[n]) + state[n][0]) for n in hs]
        if diag:
            aa = [jnp.where(before, a, 0.0) for a in aa]
        accs = [state[n][1] + jnp.dot(aa[n].astype(BF16), v_ref[pl.ds(start, T), _head_lanes(n)],
                                      preferred_element_type=F32) for n in hs]
        runs = [state[n][0] + jnp.sum(log_keeps[n], axis=-1, keepdims=True) for n in hs]
        return tuple(zip(runs, accs))

    state = tiles(i, tuple((jnp.zeros((T, 1), F32), jnp.zeros((T, LANES), F32)) for _ in hs), True)

    def cond(st):
        j, state = st
        top = state[0][0]
        for run, _ in state[1:]:
            top = jnp.maximum(top, run)
        return jnp.logical_and(j >= 0, jnp.max(top) > SB_LOG_CUTOFF)

    def step(st):
        j, state = st
        return j - 1, tiles(j, state, False)

    _, state = lax.while_loop(cond, step, (i - 1, state))
    o = _cat([acc for _, acc in state])
    o_ref[...] = (o * omul_ref[...].astype(F32)).astype(o_ref.dtype)


def stick_attention(q, k, v, omul, T, NH=STICK_HEADS):
    S = q[0].shape[0]
    width = NH * LANES
    grid = (N_HEADS // NH, S // T)
    return pl.pallas_call(
        functools.partial(_stick_body, T, NH),
        out_shape=jax.ShapeDtypeStruct((S, MIX_WIDTH), BF16),
        grid=grid,
        in_specs=[_slab_spec(T, q, _same, width), _slab_spec(None, k, _same, width),
                  _slab_spec(None, v, _same, width), _slab_spec(T, omul, _same, width)],
        out_specs=pl.BlockSpec((T, width), lambda h, i: (i, h)),
        compiler_params=_params("parallel", "arbitrary"),
        name="stick_attention",
    )(q[0], k[0], v[0], omul[0])


def _nsa_compress_body(norm, x_ref, w_ref, pos_ref, g_ref, o_ref):
    half = NSA_CMP_STRIDE * HEAD_DIM
    x = x_ref[...]
    first = jnp.dot(x, w_ref[:half, :], preferred_element_type=F32)
    second = jnp.dot(x, w_ref[half:, :], preferred_element_type=F32)
    bias = jnp.dot(pos_ref[...], w_ref[...], preferred_element_type=F32)
    n = x.shape[0]
    y = first + pltpu.roll(second, n - 1, axis=0) + bias[0:1, :]
    if norm:
        y = _head_rms(y, g_ref[...], HEAD_DIM)
    o_ref[...] = y.astype(o_ref.dtype)


def nsa_compress(x, w, pos, gain, norm):
    G, n, width = x.shape
    pos_flat = jnp.broadcast_to(pos.reshape(1, -1), (16, pos.size)).astype(BF16)
    return pl.pallas_call(
        functools.partial(_nsa_compress_body, norm),
        out_shape=jax.ShapeDtypeStruct((G, n, HEAD_DIM), BF16),
        grid=(G,),
        in_specs=[pl.BlockSpec((None, n, width), lambda g: (g, 0, 0)),
                  pl.BlockSpec(w.shape, lambda g: (0, 0)),
                  pl.BlockSpec((16, pos.size), lambda g: (0, 0)),
                  pl.BlockSpec((1, HEAD_DIM), lambda g: (0, 0))],
        out_specs=pl.BlockSpec((None, n, HEAD_DIM), lambda g: (g, 0, 0)),
        compiler_params=_params("parallel"),
        name="nsa_compress",
    )(x, w.astype(BF16), pos_flat, gain.reshape(1, HEAD_DIM))


def _nsa_cmp_body(T, n_cmp, n_sel, q_ref, kc_ref, vc_ref, c2s_ref, sl_ref, omul_ref, og_ref,
                  o_ref, sel_ref):
    g, i = pl.program_id(0), pl.program_id(1)
    NC = kc_ref.shape[0]
    kc = kc_ref[...]
    c2s = c2s_ref[...]
    vc_c2s = jnp.concatenate([vc_ref[...], c2s], axis=-1)
    og = og_ref[...]
    row = lax.broadcasted_iota(jnp.int32, (T, NC), 0)
    col = lax.broadcasted_iota(jnp.int32, (T, NC), 1)
    dist = i * T + row - (col * NSA_CMP_STRIDE + (NSA_CMP_BLOCK - 1))
    ok = (dist >= 0) & (col < n_cmp)
    dist_f = dist.astype(F32)
    imp = jnp.zeros((T, LANES), F32)
    outs = []
    for p in range(NSA_HPG):
        qp = q_ref[:, p * LANES:(p + 1) * LANES]
        s = lax.dot_general(qp, kc, (((1,), (1,)), ((), ())), preferred_element_type=F32)
        s = jnp.where(ok, s - sl_ref[p] * dist_f, NEG_BIG)
        m = jnp.max(s, axis=-1, keepdims=True)
        e = jnp.exp2(s - m)
        l = jnp.sum(e, axis=-1, keepdims=True)
        inv = jnp.where(m > 0.5 * NEG_BIG, 1.0 / l, 0.0)
        prob = e * inv
        hi = prob.astype(BF16)
        lo = (prob - hi.astype(F32)).astype(BF16)
        both = jnp.dot(hi, vc_c2s, preferred_element_type=F32)
        o = both[:, :LANES]
        imp = imp + (both[:, LANES:] + jnp.dot(lo, c2s, preferred_element_type=F32))
        o = o * _lane_col(og, g * NSA_HPG + p) * omul_ref[:, p * LANES:(p + 1) * LANES].astype(F32)
        outs.append(o.astype(o_ref.dtype))
    o_ref[...] = jnp.concatenate(outs, axis=-1)

    blk = lax.broadcasted_iota(jnp.int32, (T, LANES), 1)
    cur = (i * T + lax.broadcasted_iota(jnp.int32, (T, LANES), 0)) // NSA_SEL_BLOCK
    kn = min(NSA_SEL_N, n_sel)
    forced = ((blk == 0) | (blk == cur) | (blk == cur - 1)) & (blk <= cur)
    imp = jnp.where((blk > cur) | (blk >= n_sel) | forced, -jnp.inf, imp)
    taken = forced | _take_top(imp, blk, kn - 3)
    sel_ref[...] = jnp.where(taken, 0.0, NEG_BIG).astype(sel_ref.dtype)


def nsa_compressed(q, k_cmp, v_cmp, omul, og, T):
    S = q[0].shape[0]
    G, NC, _ = k_cmp.shape
    n_cmp = (S - NSA_CMP_BLOCK) // NSA_CMP_STRIDE + 1
    n_sel = S // NSA_SEL_BLOCK
    assert 3 <= n_sel <= LANES and NC >= n_cmp
    c_start = np.arange(NC)[:, None] * NSA_CMP_STRIDE
    s_start = np.arange(LANES)[None, :] * NSA_SEL_BLOCK
    c2s = ((c_start < s_start + NSA_SEL_BLOCK) & (c_start + NSA_CMP_BLOCK > s_start)
           & (np.arange(NC)[:, None] < n_cmp) & (np.arange(LANES)[None, :] < n_sel))
    slopes = np.broadcast_to(_alibi_slopes()[:, None, None] * LOG2E, (N_HEADS, 1, NC)).astype(np.float32)
    gw = NSA_HPG * LANES
    cmp_spec = pl.BlockSpec((None, NC, HEAD_DIM), lambda g, i: (g, 0, 0))
    return pl.pallas_call(
        functools.partial(_nsa_cmp_body, T, n_cmp, n_sel),
        out_shape=[jax.ShapeDtypeStruct((S, MIX_WIDTH), BF16),
                   jax.ShapeDtypeStruct((G, S, LANES), BF16)],
        grid=(G, S // T),
        in_specs=[_slab_spec(T, q, _same, gw), cmp_spec, cmp_spec,
                  pl.BlockSpec((NC, LANES), lambda g, i: (0, 0)),
                  pl.BlockSpec((NSA_HPG, 1, NC), lambda g, i: (g, 0, 0)),
                  _slab_spec(T, omul, _same, gw),
                  pl.BlockSpec((T, LANES), lambda g, i: (i, 0))],
        out_specs=[pl.BlockSpec((T, gw), lambda g, i: (i, g)),
                   pl.BlockSpec((None, T, LANES), lambda g, i: (g, i, 0))],
        compiler_params=_params("parallel", "parallel"),
        name="nsa_compressed",
    )(q[0], k_cmp, v_cmp, jnp.asarray(c2s.astype(np.float32), dtype=BF16), jnp.asarray(slopes),
      omul[0], og)


def _row(v):
    return v.reshape(1, -1).astype(F32)


def _rope_tables(S):
    inv_freq = ROPE_THETA ** (-jnp.arange(0, MLA_ROPE, 2, dtype=F32) / MLA_ROPE)
    ang = jnp.arange(S, dtype=F32)[:, None] * inv_freq[None, :]
    cos, sin = jnp.cos(ang), jnp.sin(ang)
    z = jnp.zeros_like(cos)
    return (jnp.concatenate([cos, z, cos, z], axis=-1),
            jnp.concatenate([-sin, z, sin, z], axis=-1))


def _rope_lanes(v):
    half = MLA_ROPE // 2
    z = jnp.zeros(v.shape[:-1] + (half,), v.dtype)
    return jnp.concatenate([v[..., :half], z, v[..., half:], z], axis=-1)


def mla_mixer(h, w_in, q_lat_norm, w_q_up, kv_lat_norm, w_kv_up, q_norm, k_norm):
    S = h.shape[0]
    c1 = MLA_Q_RANK
    c2 = c1 + MLA_KV_RANK
    c3 = c2 + MLA_ROPE
    scale = MLA_QK_DIM ** -0.5 * LOG2E
    cos, sin = _rope_tables(S)
    tables = [("cos", cos, "row"), ("sin", sin, "row")]

    w_lat = jnp.concatenate([w_in[:, :c2], _rope_lanes(w_in[:, c2:c3]),
                             jnp.zeros((w_in.shape[0], PROJ_TN - c2 - LANES), w_in.dtype)], axis=-1)
    lat = proj(h, w_lat, [(0, 1, _epi_mla_in)],
               [("g_qlat", _row(q_lat_norm), "vec"), ("g_kvlat", _row(kv_lat_norm), "vec"),
                ("gk_r", _row(_rope_lanes(k_norm[MLA_NOPE:])), "vec")] + tables, name="mla_latents")
    gate = proj(h, w_in[:, c3:], [(0, 2, _epi_silu)], name="mla_gate")

    wq = w_q_up.reshape(MLA_Q_RANK, N_HEADS, MLA_QK_DIM)
    wq = jnp.concatenate([wq[..., :MLA_NOPE], _rope_lanes(wq[..., MLA_NOPE:])], axis=-1)
    q = proj(lat, wq.reshape(MLA_Q_RANK, N_HEADS * 2 * LANES).astype(BF16), [(0, 4, _epi_mla_q(scale))],
             [("gq_n", _row(q_norm[:MLA_NOPE]), "vec"), ("gq_r", _row(_rope_lanes(q_norm[MLA_NOPE:])), "vec")]
             + tables, name="mla_q")
    wkv = w_kv_up.reshape(MLA_KV_RANK, N_HEADS, MLA_NOPE + HEAD_DIM)
    wkv = jnp.concatenate([wkv[..., :MLA_NOPE].reshape(MLA_KV_RANK, -1),
                           wkv[..., MLA_NOPE:].reshape(MLA_KV_RANK, -1)], axis=-1).astype(BF16)
    kv = proj(lat, wkv, [(0, 2, _epi_headnorm("gk_n")), (2, 4, _epi_scale(1.0))],
              [("gk_n", _row(k_norm[:MLA_NOPE]), "vec")], h_block=c1 // MLA_KV_RANK, name="mla_kv")
    k_rope = (lat, c2 // LANES)
    return [flash((q, 0), [((kv, 0), "head"), (k_rope, "all")], ((kv, N_HEADS), "head"), (gate, 0),
                  dq=2 * LANES, T=min(FLASH_T, S), name="mla_attention")]


def moba_mixer(h, w_in, q_norm, k_norm):
    S = h.shape[0]
    scale = HEAD_DIM ** -0.5 * LOG2E
    n = MIX_WIDTH // PROJ_TN
    qkvg = proj(h, w_in, [(0, n, _epi_headnorm("gq", scale)), (n, 2 * n, _epi_headnorm("gk")),
                          (2 * n, 3 * n, _epi_scale(1.0)), (3 * n, 4 * n, _epi_silu)],
                [("gq", _row(q_norm), "vec"), ("gk", _row(k_norm), "vec")], name="moba_in")
    H = N_HEADS
    q, k, v, gate = (qkvg, 0), (qkvg, H), (qkvg, 2 * H), (qkvg, 3 * H)
    bias = moba_select(q, k)
    return [flash(q, [(k, "head"), ((_block_onehot(S, MOBA_BLOCK), 0), "all")], (v, "head"), gate, dq=LANES,
                  qx=(bias, "head"), alibi=True, T=min(FLASH_T, S), name="moba_attention")]


def stick_mixer(h, w_in):
    S = h.shape[0]
    scale = HEAD_DIM ** -0.5
    n = MIX_WIDTH // PROJ_TN
    qkvg = proj(h, w_in, [(0, n, _epi_scale(scale)), (n, 3 * n, _epi_scale(1.0)), (3 * n, 4 * n, _epi_silu)],
                name="stick_in")
    H = N_HEADS
    return [stick_attention((qkvg, 0), (qkvg, H), (qkvg, 2 * H), (qkvg, 3 * H), T=min(STICK_T, S))]


def nsa_mixer(h, w_in, q_norm, k_norm, w_cmp_k, w_cmp_v, cmp_pos):
    S = h.shape[0]
    G, P, H = NSA_GROUPS, NSA_HPG, N_HEADS
    W, KV = MIX_WIDTH, NSA_KV_WIDTH
    half = KV // 2
    scale = HEAD_DIM ** -0.5 * LOG2E
    cuts = np.cumsum([W, KV, KV, KV, 3 * H]).tolist()
    assert KV == PROJ_TN and W % PROJ_TN == 0
    n = W // PROJ_TN
    main = proj(h, w_in, [(0, n, _epi_headnorm("gq", scale)), (n, n + 1, _epi_scale(1.0)),
                          (n + 1, n + 2, _epi_headnorm("gk1", n_norm=half)),
                          (n + 2, n + 3, _epi_headnorm("gk2", n_norm=half))],
                [("gq", _row(q_norm), "vec"), ("gk1", _row(k_norm[1]), "vec"), ("gk2", _row(k_norm[2]), "vec")],
                n_tiles=n + 3, name="nsa_in")
    q = (main, 0)
    kc0 = H
    ks, vs = (main, kc0 + 2 * G), (main, kc0 + 3 * G)
    kw, vw = (main, kc0 + 4 * G), (main, kc0 + 5 * G)
    wg = w_in[:, cuts[3]:cuts[4]].reshape(-1, H, 3).transpose(0, 2, 1).reshape(-1, 3 * H)
    wg = jnp.pad(wg, ((0, 0), (0, LANES - 3 * H)))
    og = proj(h, wg.astype(BF16), [(0, 1, _epi_sigmoid)], tn=LANES, out_dtype=F32, name="nsa_branch_gates")
    gate = (proj(h, w_in[:, cuts[4]:], [(0, n, _epi_silu)], name="nsa_gate"), 0)

    def chunks(t):
        return t.reshape(S // NSA_CMP_STRIDE, NSA_CMP_STRIDE, G, HEAD_DIM).transpose(2, 0, 1, 3).reshape(
            G, S // NSA_CMP_STRIDE, NSA_CMP_STRIDE * HEAD_DIM)

    c0 = kc0 * LANES
    k_cmp = nsa_compress(chunks(main[:, c0:c0 + half]), w_cmp_k, cmp_pos, k_norm[0], True)
    v_cmp = nsa_compress(chunks(main[:, c0 + half:c0 + KV]), w_cmp_v, cmp_pos, k_norm[0], False)
    o_cmp, bias = nsa_compressed(q, k_cmp, v_cmp, gate, og, T=min(NSA_CMP_T, S))
    o_slc = flash(q, [(ks, "group"), ((_block_onehot(S, NSA_SEL_BLOCK), 0), "all")], (vs, "group"), gate,
                  dq=LANES, group=P, qx=(bias, "group"), alibi=True, og=og, og_row0=H,
                  T=min(FLASH_T, S), name="nsa_selected")
    o_win = window_attention(q, kw, vw, gate, og, 2 * H, group=P, T=min(WINDOW_T, S))
    return [o_cmp, o_slc, o_win]


def kernel(x, norm_a, w_in_a, q_lat_norm_a, w_q_up_a, kv_lat_norm_a, w_kv_up_a, q_norm_a, k_norm_a, w_out_a,
           norm_b, w_in_b, q_norm_b, k_norm_b, w_out_b,
           norm_c, w_in_c, w_out_c,
           norm_d, w_in_d, q_norm_d, k_norm_d, w_cmp_k_d, w_cmp_v_d, cmp_pos_d, w_out_d):
    B, S, D = x.shape
    norms = (norm_a, norm_b, norm_c, norm_d)
    w_outs = (w_out_a, w_out_b, w_out_c, w_out_d)
    depth = norm_a.shape[0] + norm_b.shape[0] + norm_c.shape[0] + norm_d.shape[0]
    rows = x.reshape(B * S, D)
    outs = []
    for b in range(B):
        xb = rows if B == 1 else rows[b * S:(b + 1) * S]
        h = rmsnorm_rows(xb, norms[0][0])
        for layer in range(depth):
            m, j = layer % 4, layer // 4
            if m == 0:
                a = mla_mixer(h, w_in_a[j], q_lat_norm_a[j], w_q_up_a[j], kv_lat_norm_a[j], w_kv_up_a[j],
                              q_norm_a[j], k_norm_a[j])
            elif m == 1:
                a = moba_mixer(h, w_in_b[j], q_norm_b[j], k_norm_b[j])
            elif m == 2:
                a = stick_mixer(h, w_in_c[j])
            else:
                a = nsa_mixer(h, w_in_d[j], q_norm_d[j], k_norm_d[j], w_cmp_k_d[j], w_cmp_v_d[j], cmp_pos_d[j])
            nxt = layer + 1
            g_next = norms[nxt % 4][nxt // 4] if nxt < depth else None
            xb, h = outproj(a, w_outs[m][j].astype(BF16), xb, g_next)
        outs.append(xb)
    out = outs[0] if B == 1 else jnp.concatenate(outs, axis=0)
    return out.reshape(B, S, D)
```

```python
import functools

import numpy as np
import jax
import jax.numpy as jnp
from jax import lax
from jax.experimental import pallas as pl
from jax.experimental.pallas import tpu as pltpu

N_HEADS = 16
HEAD_DIM = 128
MIX_WIDTH = N_HEADS * HEAD_DIM
EPS = 1e-6
NEG_BIG = -1e30
LANES = 128

MLA_Q_RANK = 512
MLA_KV_RANK = 256
MLA_NOPE = 128
MLA_ROPE = 64
MLA_QK_DIM = MLA_NOPE + MLA_ROPE
ROPE_THETA = 10000.0

MOBA_BLOCK = 256
MOBA_TOPK = 3

NSA_GROUPS = 4
NSA_HPG = N_HEADS // NSA_GROUPS
NSA_CMP_BLOCK = 32
NSA_CMP_STRIDE = 16
NSA_SEL_BLOCK = 64
NSA_SEL_N = 16
NSA_WINDOW = 512
NSA_KV_WIDTH = 2 * NSA_GROUPS * HEAD_DIM

SB_LOG_CUTOFF = -110.0
EXP2_ZERO = 151.0

VMEM_LIMIT = 56 * 1024 * 1024

LOG2E = 1.4426950408889634

FLASH_T = 1024
WINDOW_T = 512
STICK_T = 256
NSA_CMP_T = 1024
FLASH_HEADS = 2
STICK_HEADS = 4
PROJ_TM = 1024
PROJ_TN = 1024
OUTPROJ_TM = 512

BF16 = jnp.bfloat16
F32 = jnp.float32


def _params(*sem):
    return pltpu.CompilerParams(dimension_semantics=sem, vmem_limit_bytes=VMEM_LIMIT)


def _alibi_slopes():
    return 2.0 ** (-8.0 * np.arange(1, N_HEADS + 1) / N_HEADS)


def _rmsnorm_body(x_ref, g_ref, o_ref):
    x = x_ref[...]
    ms = jnp.mean(x * x, axis=-1, keepdims=True)
    o_ref[...] = (x * lax.rsqrt(ms + EPS) * g_ref[...]).astype(o_ref.dtype)


def rmsnorm_rows(x, g, tm=512):
    S, D = x.shape
    return pl.pallas_call(
        _rmsnorm_body,
        out_shape=jax.ShapeDtypeStruct((S, D), BF16),
        grid=(S // tm,),
        in_specs=[pl.BlockSpec((tm, D), lambda i: (i, 0)),
                  pl.BlockSpec((1, D), lambda i: (0, 0))],
        out_specs=pl.BlockSpec((tm, D), lambda i: (i, 0)),
        compiler_params=_params("parallel"),
        name="rmsnorm_rows",
    )(x, g.reshape(1, D))


def _head_rms(y, gain, n_real):
    ms = jnp.sum(y * y, axis=-1, keepdims=True) * (1.0 / n_real)
    return y * lax.rsqrt(ms + EPS) * gain


def _cat(parts):
    return parts[0] if len(parts) == 1 else jnp.concatenate(parts, axis=-1)


def _epi_scale(scale):
    def epi(acc, ex):
        return acc * scale if scale != 1.0 else acc
    return epi


def _epi_silu(acc, ex):
    return acc / (1.0 + jnp.exp(-acc))


def _epi_sigmoid(acc, ex):
    return 1.0 / (1.0 + jnp.exp(-acc))


def _epi_headnorm(gain, scale=1.0, n_norm=None):
    def epi(acc, ex):
        n = acc.shape[-1] if n_norm is None else n_norm
        g = ex[gain][...] * scale
        parts = [_head_rms(acc[:, c:c + LANES], g, LANES) for c in range(0, n, LANES)]
        if n < acc.shape[-1]:
            parts.append(acc[:, n:])
        return _cat(parts)
    return epi


def _rope_slab(y, gain, cos, sin, scale):
    y = _head_rms(y, gain, MLA_ROPE)
    return (y * cos + pltpu.roll(y, 64, axis=1) * sin) * scale


def _epi_mla_q(scale):
    def epi(acc, ex):
        gn = ex["gq_n"][...] * scale
        gr, cos, sin = ex["gq_r"][...], ex["cos"][...], ex["sin"][...]
        parts = []
        for c in range(0, acc.shape[-1], 2 * LANES):
            parts.append(_head_rms(acc[:, c:c + LANES], gn, MLA_NOPE))
            parts.append(_rope_slab(acc[:, c + LANES:c + 2 * LANES], gr, cos, sin, scale))
        return _cat(parts)
    return epi


def _epi_mla_in(acc, ex):
    c1, c2 = MLA_Q_RANK, MLA_Q_RANK + MLA_KV_RANK
    q_lat = _head_rms(acc[:, :c1], ex["g_qlat"][...], c1)
    kv_lat = _head_rms(acc[:, c1:c2], ex["g_kvlat"][...], MLA_KV_RANK)
    k_rope = _rope_slab(acc[:, c2:c2 + LANES], ex["gk_r"][...], ex["cos"][...], ex["sin"][...], 1.0)
    return _cat([q_lat, kv_lat, k_rope, jnp.zeros_like(k_rope)])


def _proj_body(epis, names, cast, h_ref, w_ref, *refs):
    n = len(names)
    ex = dict(zip(names, refs[:n]))
    o_ref = refs[n]
    j, i = pl.program_id(0), pl.program_id(1)
    if cast:
        wb_ref = refs[n + 1]

        @pl.when(i == 0)
        def _():
            wb_ref[...] = w_ref[...].astype(BF16)

        w = wb_ref[...]
    else:
        w = w_ref[...]
    acc = jnp.dot(h_ref[...], w, preferred_element_type=F32)
    if len(epis) == 1:
        o_ref[...] = epis[0][2](acc, ex).astype(o_ref.dtype)
    else:
        for t0, t1, epi in epis:
            @pl.when((j >= t0) & (j < t1))
            def _(epi=epi):
                o_ref[...] = epi(acc, ex).astype(o_ref.dtype)


def proj(h, w, epis, extra=(), *, h_block=0, w_tile0=0, n_tiles=None, tn=PROJ_TN, out_dtype=BF16, name="proj"):
    S = h.shape[0]
    K = w.shape[0]
    tm = PROJ_TM
    n_tiles = (w.shape[1] // tn - w_tile0) if n_tiles is None else n_tiles
    cast = w.dtype != BF16
    in_specs = [pl.BlockSpec((tm, K), lambda j, i: (i, h_block)),
                pl.BlockSpec((K, tn), lambda j, i: (0, w_tile0 + j))]
    args = [h, w]
    names = []
    for nm, arr, kind in extra:
        names.append(nm)
        if kind == "vec":
            in_specs.append(pl.BlockSpec(arr.shape, lambda j, i: (0, 0)))
        else:
            in_specs.append(pl.BlockSpec((tm, LANES), lambda j, i: (i, 0)))
        args.append(arr)
    return pl.pallas_call(
        functools.partial(_proj_body, tuple(epis), tuple(names), cast),
        out_shape=jax.ShapeDtypeStruct((S, n_tiles * tn), out_dtype),
        grid=(n_tiles, S // tm),
        in_specs=in_specs,
        out_specs=pl.BlockSpec((tm, tn), lambda j, i: (i, j)),
        scratch_shapes=[pltpu.VMEM((K, tn), BF16)] if cast else [],
        compiler_params=_params("arbitrary", "arbitrary"),
        name=name,
    )(*args)


def _outproj_body(n_a, has_next, *refs):
    a_refs = refs[:n_a]
    w_ref, x_ref = refs[n_a], refs[n_a + 1]
    if has_next:
        g_ref, xo_ref, ho_ref = refs[n_a + 2:n_a + 5]
    else:
        xo_ref = refs[n_a + 2]
    if n_a == 1:
        a = a_refs[0][...]
    else:
        a = a_refs[0][...].astype(F32)
        for r in a_refs[1:]:
            a = a + r[...].astype(F32)
        a = a.astype(BF16)
    xn = x_ref[...] + jnp.dot(a, w_ref[...], preferred_element_type=F32)
    xo_ref[...] = xn
    if has_next:
        ms = jnp.mean(xn * xn, axis=-1, keepdims=True)
        ho_ref[...] = (xn * lax.rsqrt(ms + EPS) * g_ref[...]).astype(BF16)


def outproj(a_list, w, x, g_next=None):
    S, D = x.shape
    n_a = len(a_list)
    tm = OUTPROJ_TM // 2 if n_a > 1 else OUTPROJ_TM
    has_next = g_next is not None
    row = pl.BlockSpec((tm, D), lambda i: (i, 0))
    in_specs = [pl.BlockSpec((tm, MIX_WIDTH), lambda i: (i, 0)) for _ in a_list]
    in_specs += [pl.BlockSpec((MIX_WIDTH, D), lambda i: (0, 0)), row]
    args = list(a_list) + [w, x]
    out_shape = [jax.ShapeDtypeStruct((S, D), F32)]
    out_specs = [row]
    if has_next:
        in_specs.append(pl.BlockSpec((1, D), lambda i: (0, 0)))
        args.append(g_next.reshape(1, D))
        out_shape.append(jax.ShapeDtypeStruct((S, D), BF16))
        out_specs.append(row)
    res = pl.pallas_call(
        functools.partial(_outproj_body, n_a, has_next),
        out_shape=out_shape,
        grid=(S // tm,),
        in_specs=in_specs,
        out_specs=out_specs,
        compiler_params=_params("parallel"),
        name="outproj",
    )(*args)
    return (res[0], res[1]) if has_next else (res[0], None)


def _lane_col(x, c):
    lane = lax.broadcasted_iota(jnp.int32, x.shape, 1)
    return jnp.sum(jnp.where(lane == c, x, 0.0), axis=-1, keepdims=True)


def _head_lanes(n):
    return slice(n * LANES, (n + 1) * LANES)


def _flash_body(T, NH, dq, k_modes, v_mode, qx_mode, has_alibi, og_row0, *refs):
    pos = 0
    q_ref = refs[pos]; pos += 1
    if qx_mode is not None:
        qx_ref = refs[pos]; pos += 1
    k_refs = refs[pos:pos + len(k_modes)]; pos += len(k_modes)
    v_ref = refs[pos]; pos += 1
    if has_alibi:
        al_ref, sl_ref = refs[pos], refs[pos + 1]; pos += 2
    omul_ref = refs[pos]; pos += 1
    if og_row0 is not None:
        og_ref = refs[pos]; pos += 1
    o_ref = refs[pos]

    i = pl.program_id(1)
    hs = range(NH)
    if has_alibi:
        kmax_ref = refs[pos + 1]

        @pl.when(i == 0)
        def _():
            for n in hs:
                kn = k_refs[0][:, _head_lanes(n)] if k_modes[0] == "head" else k_refs[0][...]
                kn = kn.astype(F32)
                k2 = jnp.sum(kn * kn, axis=-1, keepdims=True)
                kmax_ref[n] = jnp.broadcast_to(jnp.max(k2, axis=0, keepdims=True), kmax_ref.shape[1:])

    qs = []
    for n in hs:
        qn = q_ref[:, n * dq:(n + 1) * dq]
        if qx_mode == "head":
            qn = jnp.concatenate([qn, qx_ref[n]], axis=-1)
        elif qx_mode == "group":
            qn = jnp.concatenate([qn, qx_ref[...]], axis=-1)
        qs.append(qn)
    dn = (((1,), (1,)), ((), ()))

    def keys(start):
        parts = [kr[pl.ds(start, T), :] for kr in k_refs]
        return [_cat([p[:, _head_lanes(n)] if mode == "head" else p for p, mode in zip(parts, k_modes)])
                for n in hs]

    ones_slab = jnp.where(lax.broadcasted_iota(jnp.int32, (T, LANES), 1) == 0, 1.0, 0.0).astype(BF16)

    def values(start):
        v = v_ref[pl.ds(start, T), :]
        return [jnp.concatenate([v[:, _head_lanes(n)] if v_mode == "head" else v, ones_slab], axis=-1)
                for n in hs]

    def tile(j, carry):
        start = pl.multiple_of(j * T, T)
        ks = keys(start)
        ss = [lax.dot_general(qs[n], ks[n], dn, preferred_element_type=F32) for n in hs]
        if has_alibi:
            off = ((j - i) * T).astype(F32)
            ss = [ss[n] + (al_ref[n] + sl_ref[n] * off) for n in hs]
        ms = [jnp.maximum(carry[n][0], jnp.max(ss[n], axis=-1, keepdims=True)) for n in hs]
        alphas = [jnp.exp2(carry[n][0] - ms[n]) for n in hs]
        ps = [jnp.exp2(ss[n] - ms[n]) for n in hs]
        vs = values(start)
        accs = [alphas[n] * carry[n][1] + jnp.dot(ps[n].astype(BF16), vs[n], preferred_element_type=F32)
                for n in hs]
        return tuple(zip(ms, accs))

    def diagonal():
        start = pl.multiple_of(i * T, T)
        ks = keys(start)
        ss = [lax.dot_general(qs[n], ks[n], dn, preferred_element_type=F32) for n in hs]
        if has_alibi:
            ss = [ss[n] + al_ref[n] for n in hs]
        rows = lax.broadcasted_iota(jnp.int32, (T, T), 0)
        cols = lax.broadcasted_iota(jnp.int32, (T, T), 1)
        ss = [jnp.where(rows >= cols, s, NEG_BIG) for s in ss]
        ms = [jnp.max(s, axis=-1, keepdims=True) for s in ss]
        ps = [jnp.exp2(s - m) for s, m in zip(ss, ms)]
        vs = values(start)
        accs = [jnp.dot(ps[n].astype(BF16), vs[n], preferred_element_type=F32) for n in hs]
        return tuple(zip(ms, accs))

    carry = diagonal()
    if has_alibi:
        reach = None
        for n in hs:
            q2 = qs[n][:, :LANES].astype(F32)
            q2 = jnp.sum(q2 * q2, axis=-1, keepdims=True)
            bound = (2.0 * jnp.sqrt(q2 * jnp.max(kmax_ref[n])) + EXP2_ZERO) / jnp.max(sl_ref[n])
            reach = bound if reach is None else jnp.maximum(reach, bound)
        reach = jnp.max(reach)
    else:
        reach = jnp.inf

    def cond(st):
        j = st[0]
        return jnp.logical_and(j >= 0, ((i - 1 - j) * T + 1).astype(F32) <= reach)

    _, carry = lax.while_loop(cond, lambda st: (st[0] - 1, tile(st[0], st[1])), (i - 1, carry))
    outs = []
    for n in hs:
        acc = carry[n][1]
        o = acc[:, :LANES] / acc[:, LANES:LANES + 1] * omul_ref[:, _head_lanes(n)].astype(F32)
        if og_row0 is not None:
            o = o * _lane_col(og_ref[...], og_row0 + pl.program_id(0) * NH + n)
        outs.append(o.astype(o_ref.dtype))
    o_ref[...] = _cat(outs)


def _slab_spec(rows, src, head_of, width=LANES):
    off = src[1] * LANES // width
    if rows is None:
        return pl.BlockSpec((src[0].shape[0], width), lambda h, i: (0, off + head_of(h)))
    return pl.BlockSpec((rows, width), lambda h, i: (i, off + head_of(h)))


def _same(h):
    return h


def _first(h):
    return 0


def _shared_spec(src, mode, NH, group):
    if mode == "head":
        return _slab_spec(None, src, _same, NH * LANES)
    if mode == "group":
        return _slab_spec(None, src, lambda hp: (hp * NH) // group)
    return _slab_spec(None, src, _first)


def _alibi_rows(n):
    sl = _alibi_slopes()[:, None, None] * LOG2E
    al = (sl * np.arange(n)[None, None, :]).astype(np.float32)
    return jnp.asarray(al), jnp.asarray(np.broadcast_to(sl, (N_HEADS, 1, n)).astype(np.float32))


def flash(q, k_parts, v, omul, *, dq, T, NH=FLASH_HEADS, group=1, qx=None, alibi=False,
          og=None, og_row0=0, name="flash"):
    S = q[0].shape[0]
    H = N_HEADS
    assert H % NH == 0 and (group == 1 or group % NH == 0)
    in_specs = [_slab_spec(T, q, _same, NH * dq)]
    args = [q[0]]
    if qx is not None:
        if qx[1] == "head":
            in_specs.append(pl.BlockSpec((NH, T, LANES), lambda hp, i: (hp, i, 0)))
        else:
            in_specs.append(pl.BlockSpec((None, T, LANES), lambda hp, i: ((hp * NH) // group, i, 0)))
        args.append(qx[0])
    for src, mode in k_parts:
        in_specs.append(_shared_spec(src, mode, NH, group))
        args.append(src[0])
    in_specs.append(_shared_spec(v[0], v[1], NH, group))
    args.append(v[0][0])
    if alibi:
        in_specs += [pl.BlockSpec((NH, 1, T), lambda hp, i: (hp, 0, 0))] * 2
        args += list(_alibi_rows(T))
    in_specs.append(_slab_spec(T, omul, _same, NH * LANES))
    args.append(omul[0])
    if og is not None:
        in_specs.append(pl.BlockSpec((T, LANES), lambda hp, i: (i, 0)))
        args.append(og)
    body = functools.partial(_flash_body, T, NH, dq, tuple(m for _, m in k_parts), v[1],
                             None if qx is None else qx[1], alibi, None if og is None else og_row0)
    return pl.pallas_call(
        body,
        out_shape=jax.ShapeDtypeStruct((S, H * LANES), BF16),
        grid=(H // NH, S // T),
        in_specs=in_specs,
        out_specs=pl.BlockSpec((T, NH * LANES), lambda hp, i: (i, hp)),
        scratch_shapes=[pltpu.VMEM((NH, 8, LANES), F32)] if alibi else [],
        compiler_params=_params("parallel", "arbitrary"),
        name=name,
    )(*args)


def _window_body(T, W, NH, og_row0, q_ref, k_ref, v_ref, al_ref, sl_ref, omul_ref, og_ref, o_ref):
    i = pl.program_id(1)
    hs = range(NH)
    span = W + T
    q0 = i * T
    start = pl.multiple_of(jnp.maximum(q0 - W, 0), T)
    k = k_ref[pl.ds(start, span), :]
    dn = (((1,), (1,)), ((), ()))
    ss = [lax.dot_general(q_ref[:, _head_lanes(n)], k, dn, preferred_element_type=F32) for n in hs]
    off = (start - q0).astype(F32)
    rows = lax.broadcasted_iota(jnp.int32, (T, span), 0)
    cols = lax.broadcasted_iota(jnp.int32, (T, span), 1)
    dist = rows - cols + (q0 - start)
    ok = (dist >= 0) & (dist < W)
    ss = [jnp.where(ok, ss[n] + (al_ref[n] + sl_ref[n] * off), NEG_BIG) for n in hs]
    ps = [jnp.exp2(s - jnp.max(s, axis=-1, keepdims=True)) for s in ss]
    ones_slab = jnp.where(lax.broadcasted_iota(jnp.int32, (span, LANES), 1) == 0, 1.0, 0.0).astype(BF16)
    v = jnp.concatenate([v_ref[pl.ds(start, span), :], ones_slab], axis=-1)
    both = [jnp.dot(p.astype(BF16), v, preferred_element_type=F32) for p in ps]
    os_ = [b[:, :LANES] for b in both]
    ls = [b[:, LANES:LANES + 1] for b in both]
    og = og_ref[...]
    outs = [(os_[n] / ls[n] * omul_ref[:, _head_lanes(n)].astype(F32)
             * _lane_col(og, og_row0 + pl.program_id(0) * NH + n)).astype(o_ref.dtype) for n in hs]
    o_ref[...] = _cat(outs)


def window_attention(q, k, v, omul, og, og_row0, *, group, T, NH=4, W=NSA_WINDOW):
    S = q[0].shape[0]
    span = W + T
    assert W % T == 0 and S >= span and group % NH == 0
    al, sl = _alibi_rows(span)
    vec = pl.BlockSpec((NH, 1, span), lambda hp, i: (hp, 0, 0))
    return pl.pallas_call(
        functools.partial(_window_body, T, W, NH, og_row0),
        out_shape=jax.ShapeDtypeStruct((S, N_HEADS * LANES), BF16),
        grid=(N_HEADS // NH, S // T),
        in_specs=[_slab_spec(T, q, _same, NH * LANES), _shared_spec(k, "group", NH, group),
                  _shared_spec(v, "group", NH, group), vec, vec, _slab_spec(T, omul, _same, NH * LANES),
                  pl.BlockSpec((T, LANES), lambda hp, i: (i, 0))],
        out_specs=pl.BlockSpec((T, NH * LANES), lambda hp, i: (i, hp)),
        compiler_params=_params("parallel", "arbitrary"),
        name="nsa_window",
    )(q[0], k[0], v[0], al, sl, omul[0], og)


def _block_onehot(S, block):
    e = (np.arange(S)[:, None] // block) == np.arange(LANES)[None, :]
    return jnp.asarray(e.astype(np.float32), dtype=BF16)


def _take_top(score, col, n):
    colf = col.astype(F32)
    left = score
    for _ in range(n):
        m = jnp.max(left, axis=-1, keepdims=True)
        idx = jnp.min(jnp.where(left == m, colf, float(LANES)), axis=-1, keepdims=True)
        left = jnp.where(colf == idx, -jnp.inf, left)
    return (left == -jnp.inf) & (score > -jnp.inf)


def _moba_select_body(T, NB, q_ref, k_ref, o_ref, km_ref):
    i = pl.program_id(1)

    @pl.when(i == 0)
    def _():
        km_ref[...] = jnp.zeros_like(km_ref)
        for b in range(NB):
            kb = k_ref[b * MOBA_BLOCK:(b + 1) * MOBA_BLOCK, :].astype(F32)
            km_ref[b:b + 1, :] = jnp.sum(kb, axis=0, keepdims=True) * (1.0 / MOBA_BLOCK)

    q = q_ref[...]
    km = km_ref[...]
    km_hi = km.astype(BF16)
    km_lo = (km - km_hi.astype(F32)).astype(BF16)
    dn = (((1,), (1,)), ((), ()))
    score = (lax.dot_general(q, km_hi, dn, preferred_element_type=F32)
             + lax.dot_general(q, km_lo, dn, preferred_element_type=F32))
    col = lax.broadcasted_iota(jnp.int32, (T, LANES), 1)
    row = lax.broadcasted_iota(jnp.int32, (T, LANES), 0)
    q_blk = (i * T + row) // MOBA_BLOCK
    score = jnp.where(col < q_blk, score, -jnp.inf)
    taken = _take_top(score, col, min(MOBA_TOPK, NB)) | (col == q_blk)
    o_ref[...] = jnp.where(taken, 0.0, NEG_BIG).astype(o_ref.dtype)


def moba_select(q, k, T=2048):
    S = q[0].shape[0]
    NB = S // MOBA_BLOCK
    return pl.pallas_call(
        functools.partial(_moba_select_body, T, NB),
        out_shape=jax.ShapeDtypeStruct((N_HEADS, S, LANES), BF16),
        grid=(N_HEADS, S // T),
        in_specs=[_slab_spec(T, q, _same), _slab_spec(None, k, _same)],
        out_specs=pl.BlockSpec((None, T, LANES), lambda h, i: (h, i, 0)),
        scratch_shapes=[pltpu.VMEM((LANES, LANES), F32)],
        compiler_params=_params("parallel", "arbitrary"),
        name="moba_select",
    )(q[0], k[0])


def _softplus(z):
    return jnp.maximum(z, 0.0) + jnp.log(1.0 + jnp.exp(-jnp.abs(z)))


def _stick_body(T, NH, q_ref, k_ref, v_ref, omul_ref, o_ref):
    i = pl.program_id(1)
    rows = lax.broadcasted_iota(jnp.int32, (T, T), 0)
    cols = lax.broadcasted_iota(jnp.int32, (T, T), 1)
    later = jnp.where(rows > cols, 1.0, 0.0).astype(BF16)
    before = cols < rows
    hs = range(NH)
    qs = [q_ref[:, _head_lanes(n)] for n in hs]
    dn = (((1,), (1,)), ((), ()))

    def tiles(j, state, diag):
        start = pl.multiple_of(j * T, T)
        zs = [lax.dot_general(qs[n], k_ref[pl.ds(start, T), _head_lanes(n)], dn,
                              preferred_element_type=F32) for n in hs]
        log_keeps = [-_softplus(z) for z in zs]
        if diag:
            log_keeps = [jnp.where(before, lk, 0.0) for lk in log_keeps]
        his = [lk.astype(BF16) for lk in log_keeps]
        los = [(lk - hi.astype(F32)).astype(BF16) for lk, hi in zip(log_keeps, his)]
        b_his = [jnp.dot(hi, later, preferred_element_type=F32) for hi in his]
        b_los = [jnp.dot(lo, later, preferred_element_type=F32) for lo in los]
        aa = [jnp.exp(zs[n] + log_keeps[n] + (b_his[n] + b_los[n]) + state[n][0]) for n in hs]
        if diag:
            aa = [jnp.where(before, a, 0.0) for a in aa]
        accs = [state[n][1] + jnp.dot(aa[n].astype(BF16), v_ref[pl.ds(start, T), _head_lanes(n)],
                                      preferred_element_type=F32) for n in hs]
        runs = [state[n][0] + jnp.sum(log_keeps[n], axis=-1, keepdims=True) for n in hs]
        return tuple(zip(runs, accs))

    state = tiles(i, tuple((jnp.zeros((T, 1), F32), jnp.zeros((T, LANES), F32)) for _ in hs), True)

    def cond(st):
        j, state = st
        top = state[0][0]
        for run, _ in state[1:]:
            top = jnp.maximum(top, run)
        return jnp.logical_and(j >= 0, jnp.max(top) > SB_LOG_CUTOFF)

    def step(st):
        j, state = st
        return j - 1, tiles(j, state, False)

    _, state = lax.while_loop(cond, step, (i - 1, state))
    o = _cat([acc for _, acc in state])
    o_ref[...] = (o * omul_ref[...].astype(F32)).astype(o_ref.dtype)


def stick_attention(q, k, v, omul, T, NH=STICK_HEADS):
    S = q[0].shape[0]
    width = NH * LANES
    grid = (N_HEADS // NH, S // T)
    return pl.pallas_call(
        functools.partial(_stick_body, T, NH),
        out_shape=jax.ShapeDtypeStruct((S, MIX_WIDTH), BF16),
        grid=grid,
        in_specs=[_slab_spec(T, q, _same, width), _slab_spec(None, k, _same, width),
                  _slab_spec(None, v, _same, width), _slab_spec(T, omul, _same, width)],
        out_specs=pl.BlockSpec((T, width), lambda h, i: (i, h)),
        compiler_params=_params("parallel", "arbitrary"),
        name="stick_attention",
    )(q[0], k[0], v[0], omul[0])


def _nsa_compress_body(norm, x_ref, w_ref, pos_ref, g_ref, o_ref):
    half = NSA_CMP_STRIDE * HEAD_DIM
    x = x_ref[...]
    first = jnp.dot(x, w_ref[:half, :], preferred_element_type=F32)
    second = jnp.dot(x, w_ref[half:, :], preferred_element_type=F32)
    bias = jnp.dot(pos_ref[...], w_ref[...], preferred_element_type=F32)
    n = x.shape[0]
    y = first + pltpu.roll(second, n - 1, axis=0) + bias[0:1, :]
    if norm:
        y = _head_rms(y, g_ref[...], HEAD_DIM)
    o_ref[...] = y.astype(o_ref.dtype)


def nsa_compress(x, w, pos, gain, norm):
    G, n, width = x.shape
    pos_flat = jnp.broadcast_to(pos.reshape(1, -1), (16, pos.size)).astype(BF16)
    return pl.pallas_call(
        functools.partial(_nsa_compress_body, norm),
        out_shape=jax.ShapeDtypeStruct((G, n, HEAD_DIM), BF16),
        grid=(G,),
        in_specs=[pl.BlockSpec((None, n, width), lambda g: (g, 0, 0)),
                  pl.BlockSpec(w.shape, lambda g: (0, 0)),
                  pl.BlockSpec((16, pos.size), lambda g: (0, 0)),
                  pl.BlockSpec((1, HEAD_DIM), lambda g: (0, 0))],
        out_specs=pl.BlockSpec((None, n, HEAD_DIM), lambda g: (g, 0, 0)),
        compiler_params=_params("parallel"),
        name="nsa_compress",
    )(x, w.astype(BF16), pos_flat, gain.reshape(1, HEAD_DIM))


def _nsa_cmp_body(T, n_cmp, n_sel, q_ref, kc_ref, vc_ref, c2s_ref, sl_ref, omul_ref, og_ref,
                  o_ref, sel_ref):
    g, i = pl.program_id(0), pl.program_id(1)
    NC = kc_ref.shape[0]
    kc = kc_ref[...]
    c2s = c2s_ref[...]
    vc_c2s = jnp.concatenate([vc_ref[...], c2s], axis=-1)
    og = og_ref[...]
    row = lax.broadcasted_iota(jnp.int32, (T, NC), 0)
    col = lax.broadcasted_iota(jnp.int32, (T, NC), 1)
    dist = i * T + row - (col * NSA_CMP_STRIDE + (NSA_CMP_BLOCK - 1))
    ok = (dist >= 0) & (col < n_cmp)
    dist_f = dist.astype(F32)
    imp = jnp.zeros((T, LANES), F32)
    outs = []
    for p in range(NSA_HPG):
        qp = q_ref[:, p * LANES:(p + 1) * LANES]
        s = lax.dot_general(qp, kc, (((1,), (1,)), ((), ())), preferred_element_type=F32)
        s = jnp.where(ok, s - sl_ref[p] * dist_f, NEG_BIG)
        m = jnp.max(s, axis=-1, keepdims=True)
        e = jnp.exp2(s - m)
        l = jnp.sum(e, axis=-1, keepdims=True)
        inv = jnp.where(m > 0.5 * NEG_BIG, 1.0 / l, 0.0)
        prob = e * inv
        hi = prob.astype(BF16)
        lo = (prob - hi.astype(F32)).astype(BF16)
        both = jnp.dot(hi, vc_c2s, preferred_element_type=F32)
        o = both[:, :LANES]
        imp = imp + (both[:, LANES:] + jnp.dot(lo, c2s, preferred_element_type=F32))
        o = o * _lane_col(og, g * NSA_HPG + p) * omul_ref[:, p * LANES:(p + 1) * LANES].astype(F32)
        outs.append(o.astype(o_ref.dtype))
    o_ref[...] = jnp.concatenate(outs, axis=-1)

    blk = lax.broadcasted_iota(jnp.int32, (T, LANES), 1)
    cur = (i * T + lax.broadcasted_iota(jnp.int32, (T, LANES), 0)) // NSA_SEL_BLOCK
    kn = min(NSA_SEL_N, n_sel)
    forced = ((blk == 0) | (blk == cur) | (blk == cur - 1)) & (blk <= cur)
    imp = jnp.where((blk > cur) | (blk >= n_sel) | forced, -jnp.inf, imp)
    taken = forced | _take_top(imp, blk, kn - 3)
    sel_ref[...] = jnp.where(taken, 0.0, NEG_BIG).astype(sel_ref.dtype)


def nsa_compressed(q, k_cmp, v_cmp, omul, og, T):
    S = q[0].shape[0]
    G, NC, _ = k_cmp.shape
    n_cmp = (S - NSA_CMP_BLOCK) // NSA_CMP_STRIDE + 1
    n_sel = S // NSA_SEL_BLOCK
    assert 3 <= n_sel <= LANES and NC >= n_cmp
    c_start = np.arange(NC)[:, None] * NSA_CMP_STRIDE
    s_start = np.arange(LANES)[None, :] * NSA_SEL_BLOCK
    c2s = ((c_start < s_start + NSA_SEL_BLOCK) & (c_start + NSA_CMP_BLOCK > s_start)
           & (np.arange(NC)[:, None] < n_cmp) & (np.arange(LANES)[None, :] < n_sel))
    slopes = np.broadcast_to(_alibi_slopes()[:, None, None] * LOG2E, (N_HEADS, 1, NC)).astype(np.float32)
    gw = NSA_HPG * LANES
    cmp_spec = pl.BlockSpec((None, NC, HEAD_DIM), lambda g, i: (g, 0, 0))
    return pl.pallas_call(
        functools.partial(_nsa_cmp_body, T, n_cmp, n_sel),
        out_shape=[jax.ShapeDtypeStruct((S, MIX_WIDTH), BF16),
                   jax.ShapeDtypeStruct((G, S, LANES), BF16)],
        grid=(G, S // T),
        in_specs=[_slab_spec(T, q, _same, gw), cmp_spec, cmp_spec,
                  pl.BlockSpec((NC, LANES), lambda g, i: (0, 0)),
                  pl.BlockSpec((NSA_HPG, 1, NC), lambda g, i: (g, 0, 0)),
                  _slab_spec(T, omul, _same, gw),
                  pl.BlockSpec((T, LANES), lambda g, i: (i, 0))],
        out_specs=[pl.BlockSpec((T, gw), lambda g, i: (i, g)),
                   pl.BlockSpec((None, T, LANES), lambda g, i: (g, i, 0))],
        compiler_params=_params("parallel", "parallel"),
        name="nsa_compressed",
    )(q[0], k_cmp, v_cmp, jnp.asarray(c2s.astype(np.float32), dtype=BF16), jnp.asarray(slopes),
      omul[0], og)


def _row(v):
    return v.reshape(1, -1).astype(F32)


def _rope_tables(S):
    inv_freq = ROPE_THETA ** (-jnp.arange(0, MLA_ROPE, 2, dtype=F32) / MLA_ROPE)
    ang = jnp.arange(S, dtype=F32)[:, None] * inv_freq[None, :]
    cos, sin = jnp.cos(ang), jnp.sin(ang)
    z = jnp.zeros_like(cos)
    return (jnp.concatenate([cos, z, cos, z], axis=-1),
            jnp.concatenate([-sin, z, sin, z], axis=-1))


def _rope_lanes(v):
    half = MLA_ROPE // 2
    z = jnp.zeros(v.shape[:-1] + (half,), v.dtype)
    return jnp.concatenate([v[..., :half], z, v[..., half:], z], axis=-1)


def mla_mixer(h, w_in, q_lat_norm, w_q_up, kv_lat_norm, w_kv_up, q_norm, k_norm):
    S = h.shape[0]
    c1 = MLA_Q_RANK
    c2 = c1 + MLA_KV_RANK
    c3 = c2 + MLA_ROPE
    scale = MLA_QK_DIM ** -0.5 * LOG2E
    cos, sin = _rope_tables(S)
    tables = [("cos", cos, "row"), ("sin", sin, "row")]

    w_lat = jnp.concatenate([w_in[:, :c2], _rope_lanes(w_in[:, c2:c3]),
                             jnp.zeros((w_in.shape[0], PROJ_TN - c2 - LANES), w_in.dtype)], axis=-1)
    lat = proj(h, w_lat, [(0, 1, _epi_mla_in)],
               [("g_qlat", _row(q_lat_norm), "vec"), ("g_kvlat", _row(kv_lat_norm), "vec"),
                ("gk_r", _row(_rope_lanes(k_norm[MLA_NOPE:])), "vec")] + tables, name="mla_latents")
    gate = proj(h, w_in[:, c3:], [(0, 2, _epi_silu)], name="mla_gate")

    wq = w_q_up.reshape(MLA_Q_RANK, N_HEADS, MLA_QK_DIM)
    wq = jnp.concatenate([wq[..., :MLA_NOPE], _rope_lanes(wq[..., MLA_NOPE:])], axis=-1)
    q = proj(lat, wq.reshape(MLA_Q_RANK, N_HEADS * 2 * LANES).astype(BF16), [(0, 4, _epi_mla_q(scale))],
             [("gq_n", _row(q_norm[:MLA_NOPE]), "vec"), ("gq_r", _row(_rope_lanes(q_norm[MLA_NOPE:])), "vec")]
             + tables, name="mla_q")
    wkv = w_kv_up.reshape(MLA_KV_RANK, N_HEADS, MLA_NOPE + HEAD_DIM)
    wkv = jnp.concatenate([wkv[..., :MLA_NOPE].reshape(MLA_KV_RANK, -1),
                           wkv[..., MLA_NOPE:].reshape(MLA_KV_RANK, -1)], axis=-1).astype(BF16)
    kv = proj(lat, wkv, [(0, 2, _epi_headnorm("gk_n")), (2, 4, _epi_scale(1.0))],
              [("gk_n", _row(k_norm[:MLA_NOPE]), "vec")], h_block=c1 // MLA_KV_RANK, name="mla_kv")
    k_rope = (lat, c2 // LANES)
    return [flash((q, 0), [((kv, 0), "head"), (k_rope, "all")], ((kv, N_HEADS), "head"), (gate, 0),
                  dq=2 * LANES, T=min(FLASH_T, S), name="mla_attention")]


def moba_mixer(h, w_in, q_norm, k_norm):
    S = h.shape[0]
    scale = HEAD_DIM ** -0.5 * LOG2E
    n = MIX_WIDTH // PROJ_TN
    qkvg = proj(h, w_in, [(0, n, _epi_headnorm("gq", scale)), (n, 2 * n, _epi_headnorm("gk")),
                          (2 * n, 3 * n, _epi_scale(1.0)), (3 * n, 4 * n, _epi_silu)],
                [("gq", _row(q_norm), "vec"), ("gk", _row(k_norm), "vec")], name="moba_in")
    H = N_HEADS
    q, k, v, gate = (qkvg, 0), (qkvg, H), (qkvg, 2 * H), (qkvg, 3 * H)
    bias = moba_select(q, k)
    return [flash(q, [(k, "head"), ((_block_onehot(S, MOBA_BLOCK), 0), "all")], (v, "head"), gate, dq=LANES,
                  qx=(bias, "head"), alibi=True, T=min(FLASH_T, S), name="moba_attention")]


def stick_mixer(h, w_in):
    S = h.shape[0]
    scale = HEAD_DIM ** -0.5
    n = MIX_WIDTH // PROJ_TN
    qkvg = proj(h, w_in, [(0, n, _epi_scale(scale)), (n, 3 * n, _epi_scale(1.0)), (3 * n, 4 * n, _epi_silu)],
                name="stick_in")
    H = N_HEADS
    return [stick_attention((qkvg, 0), (qkvg, H), (qkvg, 2 * H), (qkvg, 3 * H), T=min(STICK_T, S))]


def nsa_mixer(h, w_in, q_norm, k_norm, w_cmp_k, w_cmp_v, cmp_pos):
    S = h.shape[0]
    G, P, H = NSA_GROUPS, NSA_HPG, N_HEADS
    W, KV = MIX_WIDTH, NSA_KV_WIDTH
    half = KV // 2
    scale = HEAD_DIM ** -0.5 * LOG2E
    cuts = np.cumsum([W, KV, KV, KV, 3 * H]).tolist()
    assert KV == PROJ_TN and W % PROJ_TN == 0
    n = W // PROJ_TN
    main = proj(h, w_in, [(0, n, _epi_headnorm("gq", scale)), (n, n + 1, _epi_scale(1.0)),
                          (n + 1, n + 2, _epi_headnorm("gk1", n_norm=half)),
                          (n + 2, n + 3, _epi_headnorm("gk2", n_norm=half))],
                [("gq", _row(q_norm), "vec"), ("gk1", _row(k_norm[1]), "vec"), ("gk2", _row(k_norm[2]), "vec")],
                n_tiles=n + 3, name="nsa_in")
    q = (main, 0)
    kc0 = H
    ks, vs = (main, kc0 + 2 * G), (main, kc0 + 3 * G)
    kw, vw = (main, kc0 + 4 * G), (main, kc0 + 5 * G)
    wg = w_in[:, cuts[3]:cuts[4]].reshape(-1, H, 3).transpose(0, 2, 1).reshape(-1, 3 * H)
    wg = jnp.pad(wg, ((0, 0), (0, LANES - 3 * H)))
    og = proj(h, wg.astype(BF16), [(0, 1, _epi_sigmoid)], tn=LANES, out_dtype=F32, name="nsa_branch_gates")
    gate = (proj(h, w_in[:, cuts[4]:], [(0, n, _epi_silu)], name="nsa_gate"), 0)

    def chunks(t):
        return t.reshape(S // NSA_CMP_STRIDE, NSA_CMP_STRIDE, G, HEAD_DIM).transpose(2, 0, 1, 3).reshape(
            G, S // NSA_CMP_STRIDE, NSA_CMP_STRIDE * HEAD_DIM)

    c0 = kc0 * LANES
    k_cmp = nsa_compress(chunks(main[:, c0:c0 + half]), w_cmp_k, cmp_pos, k_norm[0], True)
    v_cmp = nsa_compress(chunks(main[:, c0 + half:c0 + KV]), w_cmp_v, cmp_pos, k_norm[0], False)
    o_cmp, bias = nsa_compressed(q, k_cmp, v_cmp, gate, og, T=min(NSA_CMP_T, S))
    o_slc = flash(q, [(ks, "group"), ((_block_onehot(S, NSA_SEL_BLOCK), 0), "all")], (vs, "group"), gate,
                  dq=LANES, group=P, qx=(bias, "group"), alibi=True, og=og, og_row0=H,
                  T=min(FLASH_T, S), name="nsa_selected")
    o_win = window_attention(q, kw, vw, gate, og, 2 * H, group=P, T=min(WINDOW_T, S))
    return [o_cmp, o_slc, o_win]


def kernel(x, norm_a, w_in_a, q_lat_norm_a, w_q_up_a, kv_lat_norm_a, w_kv_up_a, q_norm_a, k_norm_a, w_out_a,
           norm_b, w_in_b, q_norm_b, k_norm_b, w_out_b,
           norm_c, w_in_c, w_out_c,
           norm_d, w_in_d, q_norm_d, k_norm_d, w_cmp_k_d, w_cmp_v_d, cmp_pos_d, w_out_d):
    B, S, D = x.shape
    norms = (norm_a, norm_b, norm_c, norm_d)
    w_outs = (w_out_a, w_out_b, w_out_c, w_out_d)
    depth = norm_a.shape[0] + norm_b.shape[0] + norm_c.shape[0] + norm_d.shape[0]
    rows = x.reshape(B * S, D)
    outs = []
    for b in range(B):
        xb = rows if B == 1 else rows[b * S:(b + 1) * S]
        h = rmsnorm_rows(xb, norms[0][0])
        for layer in range(depth):
            m, j = layer % 4, layer // 4
            if m == 0:
                a = mla_mixer(h, w_in_a[j], q_lat_norm_a[j], w_q_up_a[j], kv_lat_norm_a[j], w_kv_up_a[j],
                              q_norm_a[j], k_norm_a[j])
            elif m == 1:
                a = moba_mixer(h, w_in_b[j], q_norm_b[j], k_norm_b[j])
            elif m == 2:
                a = stick_mixer(h, w_in_c[j])
            else:
                a = nsa_mixer(h, w_in_d[j], q_norm_d[j], k_norm_d[j], w_cmp_k_d[j], w_cmp_v_d[j], cmp_pos_d[j])
            nxt = layer + 1
            g_next = norms[nxt % 4][nxt // 4] if nxt < depth else None
            xb, h = outproj(a, w_outs[m][j].astype(BF16), xb, g_next)
        outs.append(xb)
    out = outs[0] if B == 1 else jnp.concatenate(outs, axis=0)
    return out.reshape(B, S, D)
```
